```python
import math
import jax
import jax.numpy as jnp
from jax import lax
import numpy as np

D_MODEL = 4096
BATCH = 2
SEQ = 8192
DEPTH = 4
DEC_BATCH = 2
DEC_SEQ = 4096
PAST_LEN = 128

N_MIXERS = 4
EPS = 1e-6
ADA_RANK = 256
S5_GROUP = 16
S5_GROUPS = D_MODEL // S5_GROUP
S5_STATE = 64
S5_DT_MIN = 1e-3
S5_DT_MAX = 1e-1
DA_HEADS = 16
DA_HEAD_DIM = D_MODEL // (2 * DA_HEADS)
Q_BLOCK = 128
REL_BUCKETS = 32
REL_MAX_DIST = 128
HG_HEADS = 32
HG_KEY_DIM = D_MODEL // HG_HEADS
HG_VAL_DIM = D_MODEL // HG_HEADS
HG_CHUNK = 64
POOL_WINDOWS = (2, 4, 8, 16)
POOL_GROUP = D_MODEL // len(POOL_WINDOWS)
DENSE_FF = 5632
EXPERT_FF = 1408
N_EXPERTS = 8
TOP_K = 2
N_S5 = (DEPTH + 3) // 4
N_DA = (DEPTH + 2) // 4
N_HG = (DEPTH + 1) // 4
N_POOL = DEPTH // 4
N_DENSE = (DEPTH + 1) // 2
N_MOE = DEPTH // 2

kernel_name = 'hybrid_bidir_s5_diffattn_hgrn2_pool_moe'

F32 = jnp.float32


def rmsnorm(x, g):
    xf = x.astype(F32)
    y = xf * lax.rsqrt(jnp.mean(xf * xf, axis=-1, keepdims=True) + EPS)
    return (y * g.astype(F32)).astype(x.dtype)


def _s5_combine(e1, e2):
    a1r, a1i, b1r, b1i = e1
    a2r, a2i, b2r, b2i = e2
    return (a1r * a2r - a1i * a2i, a1r * a2i + a1i * a2r,
            a2r * b1r - a2i * b1i + b2r, a2r * b1i + a2i * b1r + b2i)


def s5_mixer(u, a_re, a_im, log_dt, b_re, b_im, c_re, c_im, d_skip, w_glu):
    bsz, L, D = u.shape
    uf = u.astype(F32)
    ug = uf.reshape(bsz, L, S5_GROUPS, S5_GROUP)
    y = uf * d_skip.astype(F32)
    for dr in range(2):
        ar = a_re[dr].astype(F32)
        ai = a_im[dr].astype(F32)
        dt = jnp.exp(log_dt[dr].astype(F32))[:, None]
        mag = jnp.exp(ar * dt)
        abar_r = mag * jnp.cos(ai * dt)
        abar_i = mag * jnp.sin(ai * dt)
        den = ar * ar + ai * ai
        zr = ((abar_r - 1.0) * ar + abar_i * ai) / den
        zi = (abar_i * ar - (abar_r - 1.0) * ai) / den
        br = b_re[dr].astype(F32)
        bi = b_im[dr].astype(F32)
        bbar_r = zr[..., None] * br - zi[..., None] * bi
        bbar_i = zr[..., None] * bi + zi[..., None] * br
        bu_r = jnp.einsum('blgi,gpi->lbgp', ug, bbar_r)
        bu_i = jnp.einsum('blgi,gpi->lbgp', ug, bbar_i)
        a_shape = (L, 1, S5_GROUPS, S5_STATE)
        _, _, s_r, s_i = lax.associative_scan(
            _s5_combine,
            (jnp.broadcast_to(abar_r, a_shape), jnp.broadcast_to(abar_i, a_shape), bu_r, bu_i),
            reverse=(dr == 1), axis=0)
        y_dir = (jnp.einsum('lbgp,gip->blgi', s_r, c_re[dr].astype(F32))
                 - jnp.einsum('lbgp,gip->blgi', s_i, c_im[dr].astype(F32)))
        y = y + y_dir.reshape(bsz, L, D)
    z = jax.nn.gelu(y).astype(u.dtype)
    zz = z @ w_glu
    return zz[..., :D] * jax.nn.sigmoid(zz[..., D:])


def rel_bucket(rel):
    nb = REL_BUCKETS // 2
    max_exact = nb // 2
    n = jnp.abs(rel)
    large = max_exact + (jnp.log(jnp.maximum(n, 1).astype(F32) / max_exact)
                         / math.log(REL_MAX_DIST / max_exact) * (nb - max_exact)).astype(jnp.int32)
    large = jnp.minimum(large, nb - 1)
    return jnp.where(rel > 0, nb, 0) + jnp.where(n < max_exact, n, large)


def diff_attention(h, w_qkv, w_o, lam_q1, lam_k1, lam_q2, lam_k2, subln_g, rel_bias, lambda_init):
    bsz, L, D = h.shape
    H, dh = DA_HEADS, DA_HEAD_DIM
    nb = L // Q_BLOCK
    q, k, v = jnp.split(h @ w_qkv, 3, axis=-1)
    qb = (q * dh ** -0.5).reshape(bsz, nb, Q_BLOCK, H, 2, dh).transpose(1, 0, 3, 4, 2, 5)
    k = k.reshape(bsz, L, H, 2, dh).transpose(0, 2, 3, 1, 4)
    v = v.reshape(bsz, L, H, 2 * dh).transpose(0, 2, 1, 3)
    lam = (jnp.exp(jnp.sum(lam_q1.astype(F32) * lam_k1.astype(F32)))
           - jnp.exp(jnp.sum(lam_q2.astype(F32) * lam_k2.astype(F32))) + lambda_init)
    table = rel_bias.astype(F32)
    key_pos = jnp.arange(L, dtype=jnp.int32)

    def block(args):
        q_blk, start = args
        s = jnp.einsum('bhmqd,bhmkd->bhmqk', q_blk, k).astype(F32)
        q_pos = start + jnp.arange(Q_BLOCK, dtype=jnp.int32)
        bias = table[rel_bucket(key_pos[None, :] - q_pos[:, None])]
        s = s + jnp.transpose(bias, (2, 0, 1))[None, :, None]
        p = jax.nn.softmax(s, axis=-1)
        attn = p[:, :, 0] - lam * p[:, :, 1]
        return jnp.einsum('bhqk,bhkd->bhqd', attn.astype(v.dtype), v)

    starts = jnp.arange(nb, dtype=jnp.int32) * Q_BLOCK
    o = lax.map(block, (qb, starts))
    o = o.transpose(1, 0, 3, 2, 4).reshape(bsz, L, H, 2 * dh)
    o = rmsnorm(o, subln_g) * (1.0 - lambda_init)
    return o.reshape(bsz, L, D).astype(h.dtype) @ w_o


def hgrn2_direction(q, f_logit, inp, lb):
    bsz, L, D = q.shape
    nc = L // HG_CHUNK
    zf = f_logit.astype(F32)
    lbf = lb.astype(F32)
    log_f = jnp.logaddexp(jnp.log(lbf), jnp.log1p(-lbf) + jax.nn.log_sigmoid(zf))
    k = (1.0 - lbf) * jax.nn.sigmoid(-zf)

    def chunks(t):
        return t.astype(F32).reshape(bsz, nc, HG_CHUNK, HG_HEADS, HG_KEY_DIM).transpose(1, 0, 3, 2, 4)

    qc, kc, ic = chunks(q), chunks(k), chunks(inp)
    gc = jnp.cumsum(chunks(log_f), axis=3)
    mask = jnp.tril(jnp.ones((HG_CHUNK, HG_CHUNK), dtype=bool))[:, :, None]

    def step(state, xs):
        q_c, k_c, i_c, g_c = xs
        diff = g_c[:, :, :, None, :] - g_c[:, :, None, :, :]
        decay = jnp.exp(jnp.where(mask, diff, -jnp.inf))
        att = jnp.einsum('bhtk,bhsk,bhtsk->bhts', q_c, k_c, decay)
        o_c = (jnp.einsum('bhts,bhsv->bhtv', att, i_c)
               + jnp.einsum('bhtk,bhkv->bhtv', q_c * jnp.exp(g_c), state))
        g_last = g_c[:, :, -1:, :]
        state = (jnp.exp(g_last[:, :, 0, :, None]) * state
                 + jnp.einsum('bhsk,bhsv->bhkv', k_c * jnp.exp(g_last - g_c), i_c))
        return state, o_c

    s0 = jnp.zeros((bsz, HG_HEADS, HG_KEY_DIM, HG_VAL_DIM), F32)
    _, o = lax.scan(step, s0, (qc, kc, ic, gc))
    return o.transpose(1, 0, 3, 2, 4).reshape(bsz, L, D)


def hgrn2_mixer(h, w_in, w_o, norm_g, lb):
    bsz, L, D = h.shape
    q, f_fw, f_bw, inp, gate = jnp.split(h @ w_in, 5, axis=-1)
    o_fw = hgrn2_direction(q, f_fw, inp, lb)
    o_bw = jnp.flip(hgrn2_direction(jnp.flip(q, 1), jnp.flip(f_bw, 1), jnp.flip(inp, 1), lb), 1)
    o = rmsnorm((o_fw + o_bw).reshape(bsz, L, HG_HEADS, HG_VAL_DIM), norm_g).reshape(bsz, L, D)
    o = (o * jax.nn.silu(gate.astype(F32))).astype(h.dtype)
    return o @ w_o


def pool_mixer(h, w_pool, pool_scale):
    bsz, L, D = h.shape
    hf = h.astype(F32)
    cs = jnp.concatenate([jnp.zeros((bsz, 1, D), F32), jnp.cumsum(hf, axis=1)], axis=1)
    pos = jnp.arange(L, dtype=jnp.int32)
    outs = []
    for gi, w in enumerate(POOL_WINDOWS):
        lo = jnp.clip(pos - w // 2, 0, L)
        hi = jnp.clip(pos + w - w // 2, 0, L)
        cnt = (hi - lo).astype(F32)[None, :, None]
        csg = cs[:, :, gi * POOL_GROUP:(gi + 1) * POOL_GROUP]
        mean = (csg[:, hi] - csg[:, lo]) / cnt
        outs.append(mean - hf[:, :, gi * POOL_GROUP:(gi + 1) * POOL_GROUP])
    pooled = jnp.stack(outs, axis=2)
    y = jnp.einsum('blgi,gio->blgo', pooled, w_pool.astype(F32)).reshape(bsz, L, D)
    return (y * pool_scale.astype(F32)).astype(h.dtype)


def swiglu(h, w_in, w_out):
    g, u = jnp.split(h @ w_in, 2, axis=-1)
    return (jax.nn.silu(g) * u) @ w_out


def moe_ffn(h, w_router, w_in, w_out):
    bsz, L, D = h.shape
    t = h.reshape(bsz * L, D)
    logits = (t @ w_router).astype(F32)
    top_val, top_idx = lax.top_k(logits, TOP_K)
    gates = jax.nn.softmax(top_val, axis=-1)
    combine = jnp.sum(jax.nn.one_hot(top_idx, N_EXPERTS, dtype=F32) * gates[..., None], axis=1)
    out = jnp.zeros((bsz * L, D), F32)
    for e in range(N_EXPERTS):
        out = out + combine[:, e:e + 1] * swiglu(t, w_in[e], w_out[e]).astype(F32)
    return out.reshape(bsz, L, D).astype(h.dtype)


def trunk(x, c, norm1_g, norm2_g, final_g, w_ada_down, w_ada, b_ada,
          s5_a_re, s5_a_im, s5_log_dt, s5_b_re, s5_b_im, s5_c_re, s5_c_im, s5_d, s5_w_glu,
          da_w_qkv, da_w_o, da_lam_q1, da_lam_k1, da_lam_q2, da_lam_k2, da_subln_g, rel_bias,
          hg_w_in, hg_w_o, hg_norm_g, hg_lb_logits,
          pool_w, pool_scale,
          ff_w_in, ff_w_out, moe_router, moe_w_in, moe_w_out):
    c_low = jax.nn.silu(c) @ w_ada_down
    lb_cum = jnp.cumsum(jax.nn.softmax(hg_lb_logits.astype(F32), axis=0), axis=0)
    lb_all = jnp.concatenate([jnp.zeros_like(lb_cum[:1]), lb_cum[:-1]], axis=0)
    for l in range(DEPTH):
        mod = c_low @ w_ada[l] + b_ada[l]
        sh1, sc1, gt1, sh2, sc2, gt2 = jnp.split(mod[:, None, :], 6, axis=-1)
        h = rmsnorm(x, norm1_g[l]) * (1.0 + sc1) + sh1
        kind = l % N_MIXERS
        j = l // N_MIXERS
        if kind == 0:
            y = s5_mixer(h, s5_a_re[j], s5_a_im[j], s5_log_dt[j], s5_b_re[j], s5_b_im[j],
                         s5_c_re[j], s5_c_im[j], s5_d[j], s5_w_glu[j])
        elif kind == 1:
            y = diff_attention(h, da_w_qkv[j], da_w_o[j], da_lam_q1[j], da_lam_k1[j],
                               da_lam_q2[j], da_lam_k2[j], da_subln_g[j], rel_bias,
                               0.8 - 0.6 * math.exp(-0.3 * l))
        elif kind == 2:
            y = hgrn2_mixer(h, hg_w_in[j], hg_w_o[j], hg_norm_g[j], lb_all[l])
        else:
            y = pool_mixer(h, pool_w[j], pool_scale[j])
        x = x + (gt1 * y).astype(x.dtype)
        h = rmsnorm(x, norm2_g[l]) * (1.0 + sc2) + sh2
        if l % 2 == 0:
            y = swiglu(h, ff_w_in[l // 2], ff_w_out[l // 2])
        else:
            y = moe_ffn(h, moe_router[l // 2], moe_w_in[l // 2], moe_w_out[l // 2])
        x = x + (gt2 * y).astype(x.dtype)
    return rmsnorm(x, final_g)


def setup_inputs(seed: int = 0) -> dict:
    key = jax.random.key(seed)
    ks = jax.random.split(key, 40)
    D = D_MODEL

    def nrm(i, shape, scale):
        return jax.random.normal(ks[i], shape, F32) * scale

    s5_n = jnp.arange(S5_STATE, dtype=F32)
    return {
        'x_prompt': nrm(0, (BATCH, SEQ, D), 1.0),
        'x_sample': nrm(1, (DEC_BATCH, DEC_SEQ, D), 1.0),
        'c_prompt': nrm(2, (BATCH, D), 1.0),
        'c_sample': nrm(3, (DEC_BATCH, D), 1.0),
        'norm1_g': 1.0 + nrm(4, (DEPTH, D), 0.1),
        'norm2_g': 1.0 + nrm(5, (DEPTH, D), 0.1),
        'final_g': 1.0 + nrm(6, (D,), 0.1),
        'w_ada_down': nrm(37, (D, ADA_RANK), D ** -0.5),
        'w_ada': nrm(7, (DEPTH, ADA_RANK, 6 * D), 0.5 * ADA_RANK ** -0.5),
        'b_ada': nrm(8, (DEPTH, 6 * D), 0.02),
        's5_a_re': -0.5 + nrm(9, (N_S5, 2, S5_GROUPS, S5_STATE), 0.01),
        's5_a_im': math.pi * s5_n + nrm(10, (N_S5, 2, S5_GROUPS, S5_STATE), 0.01),
        's5_log_dt': jax.random.uniform(ks[11], (N_S5, 2, S5_GROUPS), F32,
                                        math.log(S5_DT_MIN), math.log(S5_DT_MAX)),
        's5_b_re': nrm(12, (N_S5, 2, S5_GROUPS, S5_STATE, S5_GROUP), (2 * S5_GROUP) ** -0.5),
        's5_b_im': nrm(13, (N_S5, 2, S5_GROUPS, S5_STATE, S5_GROUP), (2 * S5_GROUP) ** -0.5),
        's5_c_re': nrm(14, (N_S5, 2, S5_GROUPS, S5_GROUP, S5_STATE), (2 * S5_STATE) ** -0.5),
        's5_c_im': nrm(15, (N_S5, 2, S5_GROUPS, S5_GROUP, S5_STATE), (2 * S5_STATE) ** -0.5),
        's5_d': nrm(16, (N_S5, D), 1.0),
        's5_w_glu': nrm(17, (N_S5, D, 2 * D), D ** -0.5),
        'da_w_qkv': nrm(18, (N_DA, D, 3 * D), D ** -0.5),
        'da_w_o': nrm(19, (N_DA, D, D), D ** -0.5),
        'da_lam_q1': nrm(20, (N_DA, DA_HEAD_DIM), 0.1),
        'da_lam_k1': nrm(21, (N_DA, DA_HEAD_DIM), 0.1),
        'da_lam_q2': nrm(22, (N_DA, DA_HEAD_DIM), 0.1),
        'da_lam_k2': nrm(23, (N_DA, DA_HEAD_DIM), 0.1),
        'da_subln_g': 1.0 + nrm(24, (N_DA, 2 * DA_HEAD_DIM), 0.1),
        'rel_bias': nrm(25, (REL_BUCKETS, DA_HEADS), 0.5),
        'hg_w_in': nrm(26, (N_HG, D, 5 * D), D ** -0.5),
        'hg_w_o': nrm(27, (N_HG, D, D), D ** -0.5),
        'hg_norm_g': 1.0 + nrm(28, (N_HG, HG_VAL_DIM), 0.1),
        'hg_lb_logits': nrm(29, (DEPTH, D), 0.1),
        'pool_w': nrm(30, (N_POOL, len(POOL_WINDOWS), POOL_GROUP, POOL_GROUP), POOL_GROUP ** -0.5),
        'pool_scale': 1.0 + nrm(31, (N_POOL, D), 0.1),
        'ff_w_in': nrm(32, (N_DENSE, D, 2 * DENSE_FF), D ** -0.5),
        'ff_w_out': nrm(33, (N_DENSE, DENSE_FF, D), DENSE_FF ** -0.5),
        'moe_router': nrm(34, (N_MOE, D, N_EXPERTS), D ** -0.5),
        'moe_w_in': nrm(35, (N_MOE, N_EXPERTS, D, 2 * EXPERT_FF), D ** -0.5),
        'moe_w_out': nrm(36, (N_MOE, N_EXPERTS, EXPERT_FF, D), EXPERT_FF ** -0.5),
    }


def reference(x_prompt, x_sample, c_prompt, c_sample, norm1_g, norm2_g, final_g, w_ada_down, w_ada, b_ada,
              s5_a_re, s5_a_im, s5_log_dt, s5_b_re, s5_b_im, s5_c_re, s5_c_im, s5_d, s5_w_glu,
              da_w_qkv, da_w_o, da_lam_q1, da_lam_k1, da_lam_q2, da_lam_k2, da_subln_g, rel_bias,
              hg_w_in, hg_w_o, hg_norm_g, hg_lb_logits,
              pool_w, pool_scale,
              ff_w_in, ff_w_out, moe_router, moe_w_in, moe_w_out):
    params = (norm1_g, norm2_g, final_g, w_ada_down, w_ada, b_ada,
              s5_a_re, s5_a_im, s5_log_dt, s5_b_re, s5_b_im, s5_c_re, s5_c_im, s5_d, s5_w_glu,
              da_w_qkv, da_w_o, da_lam_q1, da_lam_k1, da_lam_q2, da_lam_k2, da_subln_g, rel_bias,
              hg_w_in, hg_w_o, hg_norm_g, hg_lb_logits,
              pool_w, pool_scale,
              ff_w_in, ff_w_out, moe_router, moe_w_in, moe_w_out)
    y_prompt = trunk(x_prompt, c_prompt, *params)
    y_sample = trunk(x_sample, c_sample, *params)
    return (y_prompt, y_sample)
```

```python
import functools
import math

import jax
import jax.numpy as jnp
from jax import lax
from jax.experimental import pallas as pl
from jax.experimental.pallas import tpu as pltpu

F32 = jnp.float32
BF16 = jnp.bfloat16
EPS = 1e-6
LANES = 128
V7X_VMEM_LIMIT = 60 * 1024 * 1024

S5_GROUP = 16
S5_CHUNK = 16
DA_HEAD_DIM = 128
REL_BUCKETS = 32
REL_MAX_DIST = 128
HG_DIM = 128
HG_CHUNK = 64
POOL_WINDOWS = (2, 4, 8, 16)
POOL_HALO = 8
TOP_K = 2
HIGHEST = lax.Precision.HIGHEST


def _params(sem, vmem=V7X_VMEM_LIMIT):
    return pltpu.CompilerParams(dimension_semantics=sem, vmem_limit_bytes=vmem)


def _tile(n, pref):
    t = min(n, pref)
    while n % t:
        t //= 2
    return t


def _small_mm_kernel(a_ref, w_ref, b_ref, o_ref, *, silu_in):
    a = a_ref[...]
    if silu_in:
        a = a * jax.nn.sigmoid(a)
    o_ref[...] = jnp.dot(a, w_ref[...], preferred_element_type=F32, precision=HIGHEST) + b_ref[...]


def small_matmul(a, w, b=None, *, silu_in=False):
    m, k = a.shape
    n = w.shape[1]
    tn = _tile(n, 2048)
    if b is None:
        b = jnp.zeros((1, n), F32)
    return pl.pallas_call(
        functools.partial(_small_mm_kernel, silu_in=silu_in),
        grid=(n // tn,),
        in_specs=[pl.BlockSpec((m, k), lambda j: (0, 0)),
                  pl.BlockSpec((k, tn), lambda j: (0, j)),
                  pl.BlockSpec((1, tn), lambda j: (0, j))],
        out_specs=pl.BlockSpec((m, tn), lambda j: (0, j)),
        out_shape=jax.ShapeDtypeStruct((m, n), F32),
        compiler_params=_params(("parallel",)),
        name="small_matmul",
    )(a, w, b.reshape(1, n))


def _norm_mod_kernel(x_ref, g_ref, sc_ref, sh_ref, o_ref):
    x = x_ref[...]
    y = x * lax.rsqrt(jnp.mean(x * x, axis=-1, keepdims=True) + EPS) * g_ref[...]
    o_ref[...] = (y * (1.0 + sc_ref[...]) + sh_ref[...]).astype(o_ref.dtype)


def _norm_mod_router_kernel(x_ref, g_ref, sc_ref, sh_ref, wr_ref, o_ref, comb_ref, *, n_experts):
    x = x_ref[...]
    y = x * lax.rsqrt(jnp.mean(x * x, axis=-1, keepdims=True) + EPS) * g_ref[...]
    h = y * (1.0 + sc_ref[...]) + sh_ref[...]
    o_ref[...] = h.astype(o_ref.dtype)
    logits = jnp.dot(h, wr_ref[...], preferred_element_type=F32, precision=HIGHEST)
    lane = lax.broadcasted_iota(jnp.int32, logits.shape, 1)
    neg = jnp.float32(-jnp.inf)
    logits = jnp.where(lane < n_experts, logits, neg)
    v1 = jnp.max(logits, axis=-1, keepdims=True)
    i1 = jnp.min(jnp.where(logits == v1, lane, LANES), axis=-1, keepdims=True)
    rest = jnp.where(lane == i1, neg, logits)
    v2 = jnp.max(rest, axis=-1, keepdims=True)
    i2 = jnp.min(jnp.where(rest == v2, lane, LANES), axis=-1, keepdims=True)
    e2 = jnp.exp(v2 - v1)
    g1 = 1.0 / (1.0 + e2)
    g2 = e2 / (1.0 + e2)
    comb_ref[...] = jnp.where(lane == i1, g1, 0.0) + jnp.where(lane == i2, g2, 0.0)


def norm_mod(x, g, mod, sc_idx, sh_idx, w_router=None):
    bsz, L, D = x.shape
    tl = _tile(L, 256)
    grid = (bsz, L // tl)
    x_spec = pl.BlockSpec((None, tl, D), lambda b, i: (b, i, 0))
    g_spec = pl.BlockSpec((1, D), lambda b, i: (0, 0))
    sc_spec = pl.BlockSpec((None, None, 1, D), lambda b, i: (b, sc_idx, 0, 0))
    sh_spec = pl.BlockSpec((None, None, 1, D), lambda b, i: (b, sh_idx, 0, 0))
    h_shape = jax.ShapeDtypeStruct((bsz, L, D), BF16)
    if w_router is None:
        return pl.pallas_call(
            _norm_mod_kernel, grid=grid,
            in_specs=[x_spec, g_spec, sc_spec, sh_spec],
            out_specs=x_spec, out_shape=h_shape,
            compiler_params=_params(("parallel", "parallel")), name="norm_mod",
        )(x, g.reshape(1, D), mod, mod)
    n_experts = w_router.shape[1]
    wr = jnp.zeros((D, LANES), F32).at[:, :n_experts].set(w_router)
    return pl.pallas_call(
        functools.partial(_norm_mod_router_kernel, n_experts=n_experts), grid=grid,
        in_specs=[x_spec, g_spec, sc_spec, sh_spec, pl.BlockSpec((D, LANES), lambda b, i: (0, 0))],
        out_specs=[x_spec, pl.BlockSpec((None, tl, LANES), lambda b, i: (b, i, 0))],
        out_shape=[h_shape, jax.ShapeDtypeStruct((bsz, L, LANES), F32)],
        compiler_params=_params(("parallel", "parallel")), name="norm_mod_router",
    )(x, g.reshape(1, D), mod, mod, wr)


def _final_norm_kernel(x_ref, g_ref, o_ref):
    x = x_ref[...]
    o_ref[...] = x * lax.rsqrt(jnp.mean(x * x, axis=-1, keepdims=True) + EPS) * g_ref[...]


def final_norm(x, g):
    bsz, L, D = x.shape
    tl = _tile(L, 256)
    spec = pl.BlockSpec((None, tl, D), lambda b, i: (b, i, 0))
    return pl.pallas_call(
        _final_norm_kernel, grid=(bsz, L // tl),
        in_specs=[spec, pl.BlockSpec((1, D), lambda b, i: (0, 0))],
        out_specs=spec, out_shape=jax.ShapeDtypeStruct((bsz, L, D), F32),
        compiler_params=_params(("parallel", "parallel")), name="final_norm",
    )(x, g.reshape(1, D))


def _mm_kernel(*refs, act, nk, has_res, has_colscale, has_rowscale):
    it = iter(refs)
    a_ref = next(it)
    w1_ref = next(it)
    w2_ref = next(it) if act else None
    res_ref = next(it) if has_res else None
    gate_ref = next(it) if has_res else None
    cs_ref = next(it) if has_colscale else None
    rs_ref = next(it) if has_rowscale else None
    o_ref = next(it)
    acc1_ref = next(it) if nk > 1 else None
    acc2_ref = next(it) if (nk > 1 and act) else None
    k = pl.program_id(2)

    a = a_ref[...]
    p1 = jnp.dot(a, w1_ref[...], preferred_element_type=F32)
    p2 = jnp.dot(a, w2_ref[...], preferred_element_type=F32) if act else None

    def finish(y1, y2):
        if act == "swiglu":
            y = y1 * jax.nn.sigmoid(y1) * y2
        elif act == "sigglu":
            y = y1 * jax.nn.sigmoid(y2)
        else:
            y = y1
        if has_rowscale:
            rs = rs_ref[...]
            lane = lax.broadcasted_iota(jnp.int32, rs.shape, 1)
            y = y * jnp.sum(jnp.where(lane == pl.program_id(1), rs, 0.0), axis=-1, keepdims=True)
        if has_colscale:
            y = y * cs_ref[...]
        if has_res:
            y = res_ref[...] + gate_ref[...] * y
        o_ref[...] = y.astype(o_ref.dtype)

    if nk == 1:
        finish(p1, p2)
    else:
        @pl.when(k == 0)
        def _():
            acc1_ref[...] = p1
            if act:
                acc2_ref[...] = p2

        @pl.when(k > 0)
        def _():
            acc1_ref[...] += p1
            if act:
                acc2_ref[...] += p2

        @pl.when(k == nk - 1)
        def _():
            finish(acc1_ref[...], acc2_ref[...] if act else None)


def _mm_call(a, weights, w_specs, n_out, *, tm, tn, nk, a_spec, act, out_dtype,
             res, gate, gate_idx, rows_per_batch, colscale, rowscale, name):
    M = a.shape[0]
    grid = (M // tm, n_out // tn, nk)
    in_specs = [a_spec] + list(w_specs)
    args = [a] + list(weights)
    if res is not None:
        tiles_per_batch = rows_per_batch // tm
        in_specs.append(pl.BlockSpec((tm, tn), lambda i, j, k: (i, j)))
        in_specs.append(pl.BlockSpec((None, None, 1, tn),
                                     lambda i, j, k: (i // tiles_per_batch, gate_idx, 0, j)))
        args += [res, gate]
    if colscale is not None:
        in_specs.append(pl.BlockSpec((1, tn), lambda i, j, k: (0, j)))
        args.append(colscale.reshape(1, n_out))
    if rowscale is not None:
        in_specs.append(pl.BlockSpec((tm, LANES), lambda i, j, k: (i, 0)))
        args.append(rowscale)
    scratch = []
    if nk > 1:
        scratch.append(pltpu.VMEM((tm, tn), F32))
        if act:
            scratch.append(pltpu.VMEM((tm, tn), F32))
    return pl.pallas_call(
        functools.partial(_mm_kernel, act=act, nk=nk, has_res=res is not None,
                          has_colscale=colscale is not None, has_rowscale=rowscale is not None),
        grid=grid, in_specs=in_specs,
        out_specs=pl.BlockSpec((tm, tn), lambda i, j, k: (i, j)),
        out_shape=jax.ShapeDtypeStruct((M, n_out), out_dtype),
        scratch_shapes=scratch,
        compiler_params=_params(("parallel", "parallel", "arbitrary")),
        name=name,
    )(*args)


def linear(a, w, *, tm=1024, tn=1024, tk=None, out_dtype=BF16, res=None, gate=None, gate_idx=0,
           rows_per_batch=None, name="linear"):
    M, K = a.shape
    N = w.shape[1]
    tm, tn = _tile(M if rows_per_batch is None else rows_per_batch, tm), _tile(N, tn)
    tk = K if tk is None else tk
    nk = K // tk
    return _mm_call(a, [w], [pl.BlockSpec((tk, tn), lambda i, j, k: (k, j))], N,
                    tm=tm, tn=tn, nk=nk, a_spec=pl.BlockSpec((tm, tk), lambda i, j, k: (i, k)),
                    act=None, out_dtype=out_dtype, res=res, gate=gate, gate_idx=gate_idx,
                    rows_per_batch=rows_per_batch, colscale=None, rowscale=None, name=name)


def glu_linear(a, w, act, *, tm=1024, tn=512, out_dtype=BF16, res=None, gate=None, gate_idx=0,
               rows_per_batch=None, name="glu_linear"):
    M, K = a.shape
    F = w.shape[1] // 2
    tm, tn = _tile(M if rows_per_batch is None else rows_per_batch, tm), _tile(F, tn)
    off = F // tn
    specs = [pl.BlockSpec((K, tn), lambda i, j, k: (0, j)),
             pl.BlockSpec((K, tn), lambda i, j, k: (0, j + off))]
    return _mm_call(a, [w, w], specs, F, tm=tm, tn=tn, nk=1,
                    a_spec=pl.BlockSpec((tm, K), lambda i, j, k: (i, 0)),
                    act=act, out_dtype=out_dtype, res=res, gate=gate, gate_idx=gate_idx,
                    rows_per_batch=rows_per_batch, colscale=None, rowscale=None, name=name)


def moe_hidden(a, w_in, combine, *, tm=512, tk=2048, name="moe_hidden"):
    M, K = a.shape
    E, _, F2 = w_in.shape
    F = F2 // 2
    tm, tk = _tile(M, tm), _tile(K, tk)
    specs = [pl.BlockSpec((None, tk, F), lambda i, j, k: (j, k, 0)),
             pl.BlockSpec((None, tk, F), lambda i, j, k: (j, k, 1))]
    return _mm_call(a, [w_in, w_in], specs, E * F, tm=tm, tn=F, nk=K // tk,
                    a_spec=pl.BlockSpec((tm, tk), lambda i, j, k: (i, k)),
                    act="swiglu", out_dtype=BF16, res=None, gate=None, gate_idx=0,
                    rows_per_batch=None, colscale=None, rowscale=combine, name=name)


def moe_out(hid, w_out, res, gate, gate_idx, rows_per_batch, *, tm=1024, tn=1024, name="moe_out"):
    M = hid.shape[0]
    E, F, D = w_out.shape
    tm, tn = _tile(rows_per_batch, tm), _tile(D, tn)
    return _mm_call(hid, [w_out], [pl.BlockSpec((None, F, tn), lambda i, j, k: (k, 0, j))], D,
                    tm=tm, tn=tn, nk=E, a_spec=pl.BlockSpec((tm, F), lambda i, j, k: (i, k)),
                    act=None, out_dtype=F32, res=res, gate=gate, gate_idx=gate_idx,
                    rows_per_batch=rows_per_batch, colscale=None, rowscale=None, name=name)


def pool_linear(a, w_pool, colscale, res, gate, gate_idx, rows_per_batch, *, tm=1024, tn=1024,
                name="pool_linear"):
    M, D = a.shape
    NG, Dg, _ = w_pool.shape
    tm, tn = _tile(rows_per_batch, tm), _tile(Dg, tn)
    per_g = Dg // tn
    return _mm_call(a, [w_pool],
                    [pl.BlockSpec((None, Dg, tn), lambda i, j, k: (j // per_g, 0, j % per_g))], D,
                    tm=tm, tn=tn, nk=1,
                    a_spec=pl.BlockSpec((tm, Dg), lambda i, j, k: (i, j // per_g)),
                    act=None, out_dtype=F32, res=res, gate=gate, gate_idx=gate_idx,
                    rows_per_batch=rows_per_batch, colscale=colscale, rowscale=None, name=name)


def _s5_tables(a_re, a_im, log_dt, b_re, b_im, c_re, c_im):
    C = S5_CHUNK
    G, P = a_re.shape[1], a_re.shape[2]

    def per_dir(dr):
        ar, ai = a_re[dr].astype(F32), a_im[dr].astype(F32)
        dt = jnp.exp(log_dt[dr].astype(F32))[:, None]
        mag = jnp.exp(ar * dt)
        abar_r, abar_i = mag * jnp.cos(ai * dt), mag * jnp.sin(ai * dt)
        den = ar * ar + ai * ai
        zr = ((abar_r - 1.0) * ar + abar_i * ai) / den
        zi = (abar_i * ar - (abar_r - 1.0) * ai) / den
        br, bi = b_re[dr].astype(F32), b_im[dr].astype(F32)
        bbar_r = zr[..., None] * br - zi[..., None] * bi
        bbar_i = zr[..., None] * bi + zi[..., None] * br

        def power(n):
            n = jnp.asarray(n, F32)[..., None, None]
            m = jnp.exp(n * (ar * dt))
            th = n * (ai * dt)
            return m * jnp.cos(th), m * jnp.sin(th)

        cr, ci = c_re[dr].astype(F32), c_im[dr].astype(F32)
        pr, pi_ = power(jnp.arange(C))
        cz_r = cr[None] * pr[:, :, None, :] - ci[None] * pi_[:, :, None, :]
        cz_i = cr[None] * pi_[:, :, None, :] + ci[None] * pr[:, :, None, :]
        kn = (jnp.einsum("ngip,gpj->gnij", cz_r, bbar_r, precision=HIGHEST)
              - jnp.einsum("ngip,gpj->gnij", cz_i, bbar_i, precision=HIGHEST))
        s_idx = jnp.arange(C)[:, None]
        t_idx = jnp.arange(C)[None, :]
        lag = (t_idx - s_idx) if dr == 0 else (s_idx - t_idx)
        valid = lag >= 0
        kt = kn[:, jnp.clip(lag, 0, C - 1)]
        kt = jnp.where(valid[None, :, :, None, None], kt, 0.0)
        tmat = kt.transpose(0, 1, 4, 2, 3).reshape(G, C * 16, C * 16)
        e = (C - 1 - jnp.arange(C)) if dr == 0 else jnp.arange(C)
        wr_, wi_ = power(e)
        w_re = wr_[..., None] * bbar_r[None] - wi_[..., None] * bbar_i[None]
        w_im = wr_[..., None] * bbar_i[None] + wi_[..., None] * bbar_r[None]
        wv = jnp.concatenate([w_re, w_im], axis=2)
        wv = wv.transpose(1, 0, 3, 2).reshape(G, C * 16, 2 * P)
        e = (jnp.arange(C) + 1) if dr == 0 else (C - jnp.arange(C))
        orr, oi = power(e)
        o_re = cr[None] * orr[:, :, None, :] - ci[None] * oi[:, :, None, :]
        o_im = -(cr[None] * oi[:, :, None, :] + ci[None] * orr[:, :, None, :])
        om = jnp.concatenate([o_re, o_im], axis=3)
        om = om.transpose(1, 3, 0, 2).reshape(G, 2 * P, C * 16)
        return tmat, wv, om

    t0, w0, o0 = per_dir(0)
    t1, w1, o1 = per_dir(1)
    tmat = t0 + t1
    wv = jnp.concatenate([w0, w1], axis=2)
    om = jnp.concatenate([o0, o1], axis=1)
    return tmat.astype(BF16), wv.astype(BF16), om.astype(BF16)


def _s5_scan_tables(a_re, a_im, log_dt, nlev):
    outs_r, outs_i = [], []
    for dr in range(2):
        ar, ai = a_re[dr].astype(F32), a_im[dr].astype(F32)
        dt = jnp.exp(log_dt[dr].astype(F32))[:, None]
        n = (S5_CHUNK * (2.0 ** jnp.arange(nlev, dtype=F32)))[:, None, None]
        m = jnp.exp(n * (ar * dt)[None])
        th = n * (ai * dt)[None]
        zr, zi = m * jnp.cos(th), m * jnp.sin(th)
        outs_r.append(jnp.concatenate([zr, zr], axis=-1))
        outs_i.append(jnp.concatenate([-zi, zi], axis=-1))
    zr = jnp.concatenate(outs_r, axis=-1).transpose(1, 0, 2)
    zi = jnp.concatenate(outs_i, axis=-1).transpose(1, 0, 2)
    return zr, zi


def _s5_kernel(x_ref, wv_ref, tm_ref, om_ref, zr_ref, zi_ref, y_ref, *, bsz, nc, nlev, p2):
    x = x_ref[...]
    v = jnp.dot(x, wv_ref[...], preferred_element_type=F32)
    row = lax.broadcasted_iota(jnp.int32, (nc, p2), 0)
    states = []
    for b in range(bsz):
        vb = v[b * nc:(b + 1) * nc]
        pf, pb = vb[:, :p2], vb[:, p2:]
        for k in range(nlev):
            d = 1 << k
            if d >= nc:
                break
            zr = zr_ref[k:k + 1, :]
            zi = zi_ref[k:k + 1, :]
            sf = jnp.where(row >= d, pltpu.roll(pf, d, 0), 0.0)
            pf = pf + zr[:, :p2] * sf + zi[:, :p2] * pltpu.roll(sf, p2 // 2, 1)
            sb = jnp.where(row < nc - d, pltpu.roll(pb, nc - d, 0), 0.0)
            pb = pb + zr[:, p2:] * sb + zi[:, p2:] * pltpu.roll(sb, p2 // 2, 1)
        sf = jnp.where(row >= 1, pltpu.roll(pf, 1, 0), 0.0)
        sb = jnp.where(row < nc - 1, pltpu.roll(pb, nc - 1, 0), 0.0)
        states.append(jnp.concatenate([sf, sb], axis=1))
    s = jnp.concatenate(states, axis=0).astype(BF16)
    y = jnp.dot(x, tm_ref[...], preferred_element_type=F32)
    y = y + jnp.dot(s, om_ref[...], preferred_element_type=F32)
    y_ref[...] = y


def s5_ssm(h, tables, scan_tables):
    bsz, L, D = h.shape
    C = S5_CHUNK
    G = D // S5_GROUP
    nc = L // C
    tmat, wv, om = tables
    zr, zi = scan_tables
    nlev = zr.shape[1]
    p2 = wv.shape[2] // 2
    cw = C * S5_GROUP
    xg = h.reshape(bsz, nc, C, G, S5_GROUP).transpose(3, 0, 1, 2, 4).reshape(G, bsz * nc, cw)
    y = pl.pallas_call(
        functools.partial(_s5_kernel, bsz=bsz, nc=nc, nlev=nlev, p2=p2),
        grid=(G,),
        in_specs=[pl.BlockSpec((None, bsz * nc, cw), lambda g: (g, 0, 0)),
                  pl.BlockSpec((None, cw, 2 * p2), lambda g: (g, 0, 0)),
                  pl.BlockSpec((None, cw, cw), lambda g: (g, 0, 0)),
                  pl.BlockSpec((None, 2 * p2, cw), lambda g: (g, 0, 0)),
                  pl.BlockSpec((None, nlev, 2 * p2), lambda g: (g, 0, 0)),
                  pl.BlockSpec((None, nlev, 2 * p2), lambda g: (g, 0, 0))],
        out_specs=pl.BlockSpec((None, bsz * nc, cw), lambda g: (g, 0, 0)),
        out_shape=jax.ShapeDtypeStruct((G, bsz * nc, cw), F32),
        compiler_params=_params(("parallel",)),
        name="s5_ssm",
    )(xg, wv, tmat, om, zr, zi)
    return y.reshape(G, bsz, nc, C, S5_GROUP).transpose(1, 2, 3, 0, 4).reshape(bsz, L, D)


def _gelu_skip_kernel(h_ref, y_ref, d_ref, o_ref):
    y = h_ref[...].astype(F32) * d_ref[...] + y_ref[...]
    o_ref[...] = jax.nn.gelu(y, approximate=True).astype(o_ref.dtype)


def gelu_skip(h, y, d_skip):
    bsz, L, D = h.shape
    tl = _tile(L, 512)
    spec = pl.BlockSpec((None, tl, D), lambda b, i: (b, i, 0))
    return pl.pallas_call(
        _gelu_skip_kernel, grid=(bsz, L // tl),
        in_specs=[spec, spec, pl.BlockSpec((1, D), lambda b, i: (0, 0))],
        out_specs=spec, out_shape=jax.ShapeDtypeStruct((bsz, L, D), BF16),
        compiler_params=_params(("parallel", "parallel")), name="gelu_skip",
    )(h, y, d_skip.reshape(1, D).astype(F32))


def _rel_bucket(rel):
    nb = REL_BUCKETS // 2
    max_exact = nb // 2
    n = jnp.abs(rel)
    large = max_exact + (jnp.log(jnp.maximum(n, 1).astype(F32) / max_exact)
                         / math.log(REL_MAX_DIST / max_exact) * (nb - max_exact)).astype(jnp.int32)
    large = jnp.minimum(large, nb - 1)
    return jnp.where(rel > 0, nb, 0) + jnp.where(n < max_exact, n, large)


def _bias_tiles_kernel(table_ref, bucket_ref, o_ref):
    h = pl.program_id(0)
    bk = bucket_ref[...]
    acc = jnp.zeros(bk.shape, F32)
    for b in range(REL_BUCKETS):
        acc = jnp.where(bk == b, table_ref[b, h], acc)
    o_ref[...] = acc


def bias_tiles(rel_bias, t):
    assert t >= REL_MAX_DIST
    H = rel_bias.shape[1]
    off = jnp.arange(-2, 3, dtype=jnp.int32)[:, None, None] * t
    rel = off + jnp.arange(t, dtype=jnp.int32)[None, None, :] - jnp.arange(t, dtype=jnp.int32)[None, :, None]
    buckets = _rel_bucket(rel)
    return pl.pallas_call(
        _bias_tiles_kernel, grid=(H, 5),
        in_specs=[pl.BlockSpec(memory_space=pltpu.SMEM),
                  pl.BlockSpec((None, t, t), lambda h, d: (d, 0, 0))],
        out_specs=pl.BlockSpec((None, None, t, t), lambda h, d: (h, d, 0, 0)),
        out_shape=jax.ShapeDtypeStruct((H, 5, t, t), F32),
        compiler_params=_params(("parallel", "parallel")), name="bias_tiles",
    )(rel_bias.astype(F32), buckets)


def _attn_kernel(q_ref, k_ref, v_ref, bias_ref, lam_ref, g_ref, o_ref, m_ref, l_ref, acc_ref, *,
                 nkv, scale, lambda_init):
    j = pl.program_id(3)

    @pl.when(j == 0)
    def _():
        m_ref[...] = jnp.full(m_ref.shape, -jnp.inf, F32)
        l_ref[...] = jnp.zeros(l_ref.shape, F32)
        acc_ref[...] = jnp.zeros(acc_ref.shape, F32)

    q = q_ref[...]
    kk = k_ref[...]
    v = v_ref[...]
    bias = bias_ref[...]
    dh = DA_HEAD_DIM
    for m in range(2):
        s = lax.dot_general(q[:, m * dh:(m + 1) * dh], kk[:, m * dh:(m + 1) * dh],
                            (((1,), (1,)), ((), ())), preferred_element_type=F32)
        s = s * scale + bias
        m_prev = m_ref[m]
        m_new = jnp.maximum(m_prev, jnp.max(s, axis=-1, keepdims=True))
        alpha = jnp.exp(m_prev - m_new)
        p = jnp.exp(s - m_new)
        l_ref[m] = alpha * l_ref[m] + jnp.sum(p, axis=-1, keepdims=True)
        acc_ref[m] = alpha * acc_ref[m] + jnp.dot(p.astype(v.dtype), v, preferred_element_type=F32)
        m_ref[m] = m_new

    @pl.when(j == nkv - 1)
    def _():
        lam_p = lam_ref[...]
        lam = (jnp.exp(jnp.sum(lam_p[0:1] * lam_p[1:2], axis=-1, keepdims=True))
               - jnp.exp(jnp.sum(lam_p[2:3] * lam_p[3:4], axis=-1, keepdims=True)) + lambda_init)
        o = acc_ref[0] / l_ref[0] - lam * (acc_ref[1] / l_ref[1])
        o = o * lax.rsqrt(jnp.mean(o * o, axis=-1, keepdims=True) + EPS) * g_ref[...]
        o_ref[...] = (o * (1.0 - lambda_init)).astype(o_ref.dtype)


def diff_attention_core(qkv, bias, lam_params, subln_g, lambda_init, t):
    bsz, L, D3 = qkv.shape
    D = D3 // 3
    hw = 2 * DA_HEAD_DIM
    H = D // hw
    nq = L // t
    return pl.pallas_call(
        functools.partial(_attn_kernel, nkv=nq, scale=DA_HEAD_DIM ** -0.5, lambda_init=lambda_init),
        grid=(bsz, H, nq, nq),
        in_specs=[pl.BlockSpec((None, t, hw), lambda b, h, i, j: (b, i, h)),
                  pl.BlockSpec((None, t, hw), lambda b, h, i, j: (b, j, H + h)),
                  pl.BlockSpec((None, t, hw), lambda b, h, i, j: (b, j, 2 * H + h)),
                  pl.BlockSpec((None, None, t, t),
                               lambda b, h, i, j: (h, jnp.clip(j - i, -2, 2) + 2, 0, 0)),
                  pl.BlockSpec((4, DA_HEAD_DIM), lambda b, h, i, j: (0, 0)),
                  pl.BlockSpec((1, hw), lambda b, h, i, j: (0, 0))],
        out_specs=pl.BlockSpec((None, t, hw), lambda b, h, i, j: (b, i, h)),
        out_shape=jax.ShapeDtypeStruct((bsz, L, D), BF16),
        scratch_shapes=[pltpu.VMEM((2, t, 1), F32), pltpu.VMEM((2, t, 1), F32),
                        pltpu.VMEM((2, t, hw), F32)],
        compiler_params=_params(("parallel", "parallel", "parallel", "arbitrary")),
        name="diff_attention",
    )(qkv, qkv, qkv, bias, lam_params, subln_g.reshape(1, hw).astype(F32))


def _hgrn_kernel(q_ref, f_ref, i_ref, gate_ref, lb_ref, ng_ref, o_ref, ofw_ref, s_ref, *,
                 tl, n_tiles):
    dr = pl.program_id(2)
    t = pl.program_id(3)
    C = HG_CHUNK
    n_chunks = tl // C

    @pl.when(t == 0)
    def _():
        s_ref[...] = jnp.zeros(s_ref.shape, F32)

    lb = lb_ref[...]
    row = lax.broadcasted_iota(jnp.int32, (C, C), 0)
    col = lax.broadcasted_iota(jnp.int32, (C, C), 1)
    fwd = dr == 0
    causal = jnp.where(fwd, row - col, col - row) >= 0
    tri = causal.astype(F32)
    tile_idx = jnp.where(fwd, t, n_tiles - 1 - t)
    base = pl.multiple_of(tile_idx * tl, C)

    def chunk(c, carry):
        ci = jnp.where(fwd, c, n_chunks - 1 - c)
        r0 = pl.multiple_of(ci * C, C)
        q = q_ref[pl.ds(r0, C), :]
        z = f_ref[pl.ds(r0, C), :]
        inp = i_ref[pl.ds(r0, C), :].astype(BF16)
        sig = jax.nn.sigmoid(z)
        logf = jnp.log(lb + (1.0 - lb) * sig)
        kk = (1.0 - lb) * (1.0 - sig)
        g = jnp.dot(tri, logf, preferred_element_type=F32, precision=HIGHEST)
        g_tot = jnp.sum(logf, axis=0, keepdims=True)
        qt = (q * jnp.exp(g)).astype(BF16)
        kt = (kk * jnp.exp(-g)).astype(BF16)
        att = lax.dot_general(qt, kt, (((1,), (1,)), ((), ())), preferred_element_type=F32)
        att = jnp.where(causal, att, 0.0).astype(BF16)
        st_old = s_ref[...]
        o = (jnp.dot(att, inp, preferred_element_type=F32)
             + lax.dot_general(qt, st_old.astype(BF16), (((1,), (1,)), ((), ())),
                               preferred_element_type=F32))
        kd = (kk * jnp.exp(g_tot - g)).astype(BF16)
        s_ref[...] = (jnp.exp(g_tot) * st_old
                      + lax.dot_general(inp, kd, (((0,), (0,)), ((), ())), preferred_element_type=F32))
        rows = pl.ds(pl.multiple_of(base + r0, C), C)

        @pl.when(fwd)
        def _():
            ofw_ref[rows, :] = o

        @pl.when(jnp.logical_not(fwd))
        def _():
            tot = ofw_ref[rows, :] + o
            y = tot * lax.rsqrt(jnp.mean(tot * tot, axis=-1, keepdims=True) + EPS) * ng_ref[...]
            gt = gate_ref[pl.ds(r0, C), :]
            o_ref[pl.ds(r0, C), :] = (y * (gt * jax.nn.sigmoid(gt))).astype(o_ref.dtype)
        return carry

    lax.fori_loop(0, n_chunks, chunk, 0)


def hgrn2_core(proj, lb, norm_g, *, tl=512):
    bsz, L, D5 = proj.shape
    D = D5 // 5
    H = D // HG_DIM
    tl = _tile(L, tl)
    nt = L // tl

    def tile_of(dr, t):
        return jnp.where(dr == 0, t, nt - 1 - t)

    def in_spec(section):
        return pl.BlockSpec((None, tl, HG_DIM),
                            lambda b, h, dr, t: (b, tile_of(dr, t), section * H + h))

    f_spec = pl.BlockSpec((None, tl, HG_DIM),
                          lambda b, h, dr, t: (b, tile_of(dr, t), (1 + dr) * H + h))
    out_spec = pl.BlockSpec((None, tl, HG_DIM),
                            lambda b, h, dr, t: (b, jnp.where(dr == 0, nt - 1, nt - 1 - t), h))
    return pl.pallas_call(
        functools.partial(_hgrn_kernel, tl=tl, n_tiles=nt),
        grid=(bsz, H, 2, nt),
        in_specs=[in_spec(0), f_spec, in_spec(3), in_spec(4),
                  pl.BlockSpec((1, HG_DIM), lambda b, h, dr, t: (0, h)),
                  pl.BlockSpec((1, HG_DIM), lambda b, h, dr, t: (0, 0))],
        out_specs=out_spec,
        out_shape=jax.ShapeDtypeStruct((bsz, L, D), BF16),
        scratch_shapes=[pltpu.VMEM((L, HG_DIM), F32), pltpu.VMEM((HG_DIM, HG_DIM), F32)],
        compiler_params=_params(("parallel", "parallel", "arbitrary", "arbitrary")),
        name="hgrn2",
    )(proj, proj, proj, proj, lb.reshape(1, D).astype(F32), norm_g.reshape(1, HG_DIM).astype(F32))


def _pool_kernel(h_ref, o_ref, pad_ref, *, L):
    grp = pl.program_id(1)
    x = h_ref[...].astype(F32)
    zeros = jnp.zeros((POOL_HALO, x.shape[1]), F32)
    pad_ref[0:POOL_HALO, :] = zeros
    pad_ref[POOL_HALO:POOL_HALO + L, :] = x
    pad_ref[POOL_HALO + L:POOL_HALO + L + POOL_HALO, :] = zeros
    pos = lax.broadcasted_iota(jnp.int32, (L, 1), 0)
    for gi, w in enumerate(POOL_WINDOWS):
        @pl.when(grp == gi)
        def _(w=w):
            acc = jnp.zeros(x.shape, F32)
            for d in range(-(w // 2), w - w // 2):
                acc = acc + pad_ref[POOL_HALO + d:POOL_HALO + d + L, :]
            lo = jnp.clip(pos - w // 2, 0, L)
            hi = jnp.clip(pos + w - w // 2, 0, L)
            cnt = (hi - lo).astype(F32)
            o_ref[...] = (acc / cnt - x).astype(o_ref.dtype)


def pool_core(h, *, width=128):
    bsz, L, D = h.shape
    ng = len(POOL_WINDOWS)
    dg = D // ng
    width = _tile(dg, width)
    per_g = dg // width
    spec = pl.BlockSpec((None, L, width), lambda b, g, s: (b, 0, g * per_g + s))
    return pl.pallas_call(
        functools.partial(_pool_kernel, L=L),
        grid=(bsz, ng, per_g),
        in_specs=[spec], out_specs=spec,
        out_shape=jax.ShapeDtypeStruct((bsz, L, D), BF16),
        scratch_shapes=[pltpu.VMEM((L + 2 * POOL_HALO, width), F32)],
        compiler_params=_params(("parallel", "parallel", "parallel")),
        name="pool",
    )(h)


def _trunk(x, c, p, shared):
    bsz, L, D = x.shape
    T = bsz * L
    depth = p["norm1_g"].shape[0]
    c_low = small_matmul(c, p["w_ada_down"], silu_in=True)
    for l in range(depth):
        mod = small_matmul(c_low, p["w_ada"][l], p["b_ada"][l]).reshape(bsz, 6, 1, D)
        h = norm_mod(x, p["norm1_g"][l], mod, 1, 0)
        kind, j = l % 4, l // 4
        x2 = x.reshape(T, D)
        if kind == 0:
            y = s5_ssm(h, shared["s5_tables"][j], shared["s5_scan"][(j, L)])
            z = gelu_skip(h, y, p["s5_d"][j])
            x = glu_linear(z.reshape(T, D), shared["s5_w_glu"][j], "sigglu", out_dtype=F32,
                           res=x2, gate=mod, gate_idx=2, rows_per_batch=L, name="s5_glu")
        elif kind == 1:
            lambda_init = 0.8 - 0.6 * math.exp(-0.3 * l)
            qkv = linear(h.reshape(T, D), shared["da_w_qkv"][j], name="da_qkv")
            lam_params = jnp.stack([p["da_lam_q1"][j], p["da_lam_k1"][j],
                                    p["da_lam_q2"][j], p["da_lam_k2"][j]]).astype(F32)
            o = diff_attention_core(qkv.reshape(bsz, L, 3 * D), shared["bias_tiles"], lam_params,
                                    p["da_subln_g"][j], lambda_init, shared["attn_tile"])
            x = linear(o.reshape(T, D), shared["da_w_o"][j], out_dtype=F32, res=x2, gate=mod,
                       gate_idx=2, rows_per_batch=L, name="da_out")
        elif kind == 2:
            proj = linear(h.reshape(T, D), shared["hg_w_in"][j], out_dtype=F32, name="hg_in")
            o = hgrn2_core(proj.reshape(bsz, L, 5 * D), shared["hg_lb"][l], p["hg_norm_g"][j])
            x = linear(o.reshape(T, D), shared["hg_w_o"][j], out_dtype=F32, res=x2, gate=mod,
                       gate_idx=2, rows_per_batch=L, name="hg_out")
        else:
            pooled = pool_core(h)
            x = pool_linear(pooled.reshape(T, D), shared["pool_w"][j], p["pool_scale"][j].astype(F32),
                            x2, mod, 2, L)
        x = x.reshape(bsz, L, D)
        x2 = x.reshape(T, D)
        if l % 2 == 0:
            h = norm_mod(x, p["norm2_g"][l], mod, 4, 3)
            hid = glu_linear(h.reshape(T, D), shared["ff_w_in"][l // 2], "swiglu", name="ff_in")
            x = linear(hid, shared["ff_w_out"][l // 2], tk=_ff_tk(hid.shape[1]), out_dtype=F32,
                       res=x2, gate=mod, gate_idx=5, rows_per_batch=L, name="ff_out")
        else:
            h, comb = norm_mod(x, p["norm2_g"][l], mod, 4, 3, w_router=p["moe_router"][l // 2])
            hid = moe_hidden(h.reshape(T, D), shared["moe_w_in"][l // 2], comb.reshape(T, LANES))
            x = moe_out(hid, shared["moe_w_out"][l // 2], x2, mod, 5, L)
        x = x.reshape(bsz, L, D)
    return final_norm(x, p["final_g"])


def _ff_tk(k):
    return k // 2 if (k // 2) % LANES == 0 else k


def kernel(x_prompt, x_sample, c_prompt, c_sample, norm1_g, norm2_g, final_g, w_ada_down, w_ada, b_ada,
           s5_a_re, s5_a_im, s5_log_dt, s5_b_re, s5_b_im, s5_c_re, s5_c_im, s5_d, s5_w_glu,
           da_w_qkv, da_w_o, da_lam_q1, da_lam_k1, da_lam_q2, da_lam_k2, da_subln_g, rel_bias,
           hg_w_in, hg_w_o, hg_norm_g, hg_lb_logits,
           pool_w, pool_scale,
           ff_w_in, ff_w_out, moe_router, moe_w_in, moe_w_out):
    p = dict(norm1_g=norm1_g, norm2_g=norm2_g, final_g=final_g, w_ada_down=w_ada_down, w_ada=w_ada,
             b_ada=b_ada, s5_d=s5_d, da_lam_q1=da_lam_q1, da_lam_k1=da_lam_k1, da_lam_q2=da_lam_q2,
             da_lam_k2=da_lam_k2, da_subln_g=da_subln_g, hg_norm_g=hg_norm_g, pool_scale=pool_scale,
             moe_router=moe_router)
    seq_lens = sorted({x_prompt.shape[1], x_sample.shape[1]})
    attn_tile = _tile(min(seq_lens), 512)
    lb_cum = jnp.cumsum(jax.nn.softmax(hg_lb_logits.astype(F32), axis=0), axis=0)
    hg_lb = jnp.concatenate([jnp.zeros_like(lb_cum[:1]), lb_cum[:-1]], axis=0)
    s5_scan = {}
    for j in range(s5_a_re.shape[0]):
        for L in seq_lens:
            nlev = max(1, math.ceil(math.log2(L // S5_CHUNK)))
            s5_scan[(j, L)] = _s5_scan_tables(s5_a_re[j], s5_a_im[j], s5_log_dt[j], nlev)
    shared = dict(
        s5_tables=[_s5_tables(s5_a_re[j], s5_a_im[j], s5_log_dt[j], s5_b_re[j], s5_b_im[j],
                              s5_c_re[j], s5_c_im[j]) for j in range(s5_a_re.shape[0])],
        s5_scan=s5_scan,
        s5_w_glu=s5_w_glu.astype(BF16), da_w_qkv=da_w_qkv.astype(BF16), da_w_o=da_w_o.astype(BF16),
        hg_w_in=hg_w_in.astype(BF16), hg_w_o=hg_w_o.astype(BF16), pool_w=pool_w.astype(BF16),
        ff_w_in=ff_w_in.astype(BF16), ff_w_out=ff_w_out.astype(BF16),
        moe_w_in=moe_w_in.astype(BF16), moe_w_out=moe_w_out.astype(BF16),
        hg_lb=hg_lb, attn_tile=attn_tile, bias_tiles=bias_tiles(rel_bias, attn_tile),
    )
    y_prompt = _trunk(x_prompt, c_prompt, p, shared)
    y_sample = _trunk(x_sample, c_sample, p, shared)
    return (y_prompt, y_sample)
```

```python
import functools
import math

import jax
import jax.numpy as jnp
from jax import lax
from jax.experimental import pallas as pl
from jax.experimental.pallas import tpu as pltpu

F32 = jnp.float32
BF16 = jnp.bfloat16
EPS = 1e-6
LANES = 128
V7X_VMEM_LIMIT = 60 * 1024 * 1024

S5_GROUP = 16
S5_CHUNK = 16
DA_HEAD_DIM = 128
ATTN_ROWS = 128
ATTN_UNROLL = 4
REL_BUCKETS = 32
REL_MAX_DIST = 128
HG_DIM = 128
HG_CHUNK = 64
HG_HEADS_PER_STEP = 4
POOL_WINDOWS = (2, 4, 8, 16)
POOL_HALO = 8
TOP_K = 2
HIGHEST = lax.Precision.HIGHEST
LOG2E = math.log2(math.e)


def _params(sem, vmem=V7X_VMEM_LIMIT):
    return pltpu.CompilerParams(dimension_semantics=sem, vmem_limit_bytes=vmem)


def _tile(n, pref):
    t = min(n, pref)
    while n % t:
        t //= 2
    return t


def _small_mm_kernel(a_ref, w_ref, b_ref, o_ref, *, silu_in):
    a = a_ref[...]
    if silu_in:
        a = a * jax.nn.sigmoid(a)
    o_ref[...] = jnp.dot(a, w_ref[...], preferred_element_type=F32, precision=HIGHEST) + b_ref[...]


def small_matmul(a, w, b=None, *, silu_in=False):
    m, k = a.shape
    n = w.shape[1]
    tn = _tile(n, 2048)
    if b is None:
        b = jnp.zeros((1, n), F32)
    return pl.pallas_call(
        functools.partial(_small_mm_kernel, silu_in=silu_in),
        grid=(n // tn,),
        in_specs=[pl.BlockSpec((m, k), lambda j: (0, 0)),
                  pl.BlockSpec((k, tn), lambda j: (0, j)),
                  pl.BlockSpec((1, tn), lambda j: (0, j))],
        out_specs=pl.BlockSpec((m, tn), lambda j: (0, j)),
        out_shape=jax.ShapeDtypeStruct((m, n), F32),
        compiler_params=_params(("parallel",)),
        name="small_matmul",
    )(a, w, b.reshape(1, n))


def _norm_mod_kernel(x_ref, g_ref, sc_ref, sh_ref, o_ref):
    x = x_ref[...]
    y = x * lax.rsqrt(jnp.mean(x * x, axis=-1, keepdims=True) + EPS) * g_ref[...]
    o_ref[...] = (y * (1.0 + sc_ref[...]) + sh_ref[...]).astype(o_ref.dtype)


def _norm_mod_router_kernel(x_ref, g_ref, sc_ref, sh_ref, wr_ref, o_ref, comb_ref, *, n_experts):
    x = x_ref[...]
    y = x * lax.rsqrt(jnp.mean(x * x, axis=-1, keepdims=True) + EPS) * g_ref[...]
    h = y * (1.0 + sc_ref[...]) + sh_ref[...]
    o_ref[...] = h.astype(o_ref.dtype)
    logits = jnp.dot(h, wr_ref[...], preferred_element_type=F32, precision=HIGHEST)
    lane = lax.broadcasted_iota(jnp.int32, logits.shape, 1)
    neg = jnp.float32(-jnp.inf)
    logits = jnp.where(lane < n_experts, logits, neg)
    v1 = jnp.max(logits, axis=-1, keepdims=True)
    i1 = jnp.min(jnp.where(logits == v1, lane, LANES), axis=-1, keepdims=True)
    rest = jnp.where(lane == i1, neg, logits)
    v2 = jnp.max(rest, axis=-1, keepdims=True)
    i2 = jnp.min(jnp.where(rest == v2, lane, LANES), axis=-1, keepdims=True)
    e2 = jnp.exp(v2 - v1)
    g1 = 1.0 / (1.0 + e2)
    g2 = e2 / (1.0 + e2)
    comb_ref[...] = jnp.where(lane == i1, g1, 0.0) + jnp.where(lane == i2, g2, 0.0)


def norm_mod(x, g, mod, sc_idx, sh_idx, w_router=None):
    bsz, L, D = x.shape
    tl = _tile(L, 256)
    grid = (bsz, L // tl)
    x_spec = pl.BlockSpec((None, tl, D), lambda b, i: (b, i, 0))
    g_spec = pl.BlockSpec((1, D), lambda b, i: (0, 0))
    sc_spec = pl.BlockSpec((None, None, 1, D), lambda b, i: (b, sc_idx, 0, 0))
    sh_spec = pl.BlockSpec((None, None, 1, D), lambda b, i: (b, sh_idx, 0, 0))
    h_shape = jax.ShapeDtypeStruct((bsz, L, D), BF16)
    if w_router is None:
        return pl.pallas_call(
            _norm_mod_kernel, grid=grid,
            in_specs=[x_spec, g_spec, sc_spec, sh_spec],
            out_specs=x_spec, out_shape=h_shape,
            compiler_params=_params(("parallel", "parallel")), name="norm_mod",
        )(x, g.reshape(1, D), mod, mod)
    n_experts = w_router.shape[1]
    wr = jnp.zeros((D, LANES), F32).at[:, :n_experts].set(w_router)
    return pl.pallas_call(
        functools.partial(_norm_mod_router_kernel, n_experts=n_experts), grid=grid,
        in_specs=[x_spec, g_spec, sc_spec, sh_spec, pl.BlockSpec((D, LANES), lambda b, i: (0, 0))],
        out_specs=[x_spec, pl.BlockSpec((None, tl, LANES), lambda b, i: (b, i, 0))],
        out_shape=[h_shape, jax.ShapeDtypeStruct((bsz, L, LANES), F32)],
        compiler_params=_params(("parallel", "parallel")), name="norm_mod_router",
    )(x, g.reshape(1, D), mod, mod, wr)


def _final_norm_kernel(x_ref, g_ref, o_ref):
    x = x_ref[...]
    o_ref[...] = x * lax.rsqrt(jnp.mean(x * x, axis=-1, keepdims=True) + EPS) * g_ref[...]


def final_norm(x, g):
    bsz, L, D = x.shape
    tl = _tile(L, 256)
    spec = pl.BlockSpec((None, tl, D), lambda b, i: (b, i, 0))
    return pl.pallas_call(
        _final_norm_kernel, grid=(bsz, L // tl),
        in_specs=[spec, pl.BlockSpec((1, D), lambda b, i: (0, 0))],
        out_specs=spec, out_shape=jax.ShapeDtypeStruct((bsz, L, D), F32),
        compiler_params=_params(("parallel", "parallel")), name="final_norm",
    )(x, g.reshape(1, D))


def _mm_kernel(*refs, act, nk, has_res, has_colscale, has_rowscale):
    it = iter(refs)
    a_ref = next(it)
    w1_ref = next(it)
    w2_ref = next(it) if act else None
    res_ref = next(it) if has_res else None
    gate_ref = next(it) if has_res else None
    cs_ref = next(it) if has_colscale else None
    rs_ref = next(it) if has_rowscale else None
    o_ref = next(it)
    acc1_ref = next(it) if nk > 1 else None
    acc2_ref = next(it) if (nk > 1 and act) else None
    k = pl.program_id(2)

    a = a_ref[...]
    p1 = jnp.dot(a, w1_ref[...], preferred_element_type=F32)
    p2 = jnp.dot(a, w2_ref[...], preferred_element_type=F32) if act else None

    def finish(y1, y2):
        if act == "swiglu":
            y = y1 * jax.nn.sigmoid(y1) * y2
        elif act == "sigglu":
            y = y1 * jax.nn.sigmoid(y2)
        else:
            y = y1
        if has_rowscale:
            rs = rs_ref[...]
            lane = lax.broadcasted_iota(jnp.int32, rs.shape, 1)
            y = y * jnp.sum(jnp.where(lane == pl.program_id(1), rs, 0.0), axis=-1, keepdims=True)
        if has_colscale:
            y = y * cs_ref[...]
        if has_res:
            y = res_ref[...] + gate_ref[...] * y
        o_ref[...] = y.astype(o_ref.dtype)

    if nk == 1:
        finish(p1, p2)
    else:
        @pl.when(k == 0)
        def _():
            acc1_ref[...] = p1
            if act:
                acc2_ref[...] = p2

        @pl.when(k > 0)
        def _():
            acc1_ref[...] += p1
            if act:
                acc2_ref[...] += p2

        @pl.when(k == nk - 1)
        def _():
            finish(acc1_ref[...], acc2_ref[...] if act else None)


def _mm_call(a, weights, w_specs, n_out, *, tm, tn, nk, a_spec, act, out_dtype,
             res, gate, gate_idx, rows_per_batch, colscale, rowscale, name):
    M = a.shape[0]
    grid = (M // tm, n_out // tn, nk)
    in_specs = [a_spec] + list(w_specs)
    args = [a] + list(weights)
    if res is not None:
        tiles_per_batch = rows_per_batch // tm
        in_specs.append(pl.BlockSpec((tm, tn), lambda i, j, k: (i, j)))
        in_specs.append(pl.BlockSpec((None, None, 1, tn),
                                     lambda i, j, k: (i // tiles_per_batch, gate_idx, 0, j)))
        args += [res, gate]
    if colscale is not None:
        in_specs.append(pl.BlockSpec((1, tn), lambda i, j, k: (0, j)))
        args.append(colscale.reshape(1, n_out))
    if rowscale is not None:
        in_specs.append(pl.BlockSpec((tm, LANES), lambda i, j, k: (i, 0)))
        args.append(rowscale)
    scratch = []
    if nk > 1:
        scratch.append(pltpu.VMEM((tm, tn), F32))
        if act:
            scratch.append(pltpu.VMEM((tm, tn), F32))
    return pl.pallas_call(
        functools.partial(_mm_kernel, act=act, nk=nk, has_res=res is not None,
                          has_colscale=colscale is not None, has_rowscale=rowscale is not None),
        grid=grid, in_specs=in_specs,
        out_specs=pl.BlockSpec((tm, tn), lambda i, j, k: (i, j)),
        out_shape=jax.ShapeDtypeStruct((M, n_out), out_dtype),
        scratch_shapes=scratch,
        compiler_params=_params(("parallel", "parallel", "arbitrary")),
        name=name,
    )(*args)


def linear(a, w, *, tm=1024, tn=1024, tk=None, out_dtype=BF16, res=None, gate=None, gate_idx=0,
           rows_per_batch=None, name="linear"):
    M, K = a.shape
    N = w.shape[1]
    tm, tn = _tile(M if rows_per_batch is None else rows_per_batch, tm), _tile(N, tn)
    tk = K if tk is None else tk
    nk = K // tk
    return _mm_call(a, [w], [pl.BlockSpec((tk, tn), lambda i, j, k: (k, j))], N,
                    tm=tm, tn=tn, nk=nk, a_spec=pl.BlockSpec((tm, tk), lambda i, j, k: (i, k)),
                    act=None, out_dtype=out_dtype, res=res, gate=gate, gate_idx=gate_idx,
                    rows_per_batch=rows_per_batch, colscale=None, rowscale=None, name=name)


def glu_linear(a, w, act, *, tm=1024, tn=512, out_dtype=BF16, res=None, gate=None, gate_idx=0,
               rows_per_batch=None, name="glu_linear"):
    M, K = a.shape
    F = w.shape[1] // 2
    tm, tn = _tile(M if rows_per_batch is None else rows_per_batch, tm), _tile(F, tn)
    off = F // tn
    specs = [pl.BlockSpec((K, tn), lambda i, j, k: (0, j)),
             pl.BlockSpec((K, tn), lambda i, j, k: (0, j + off))]
    return _mm_call(a, [w, w], specs, F, tm=tm, tn=tn, nk=1,
                    a_spec=pl.BlockSpec((tm, K), lambda i, j, k: (i, 0)),
                    act=act, out_dtype=out_dtype, res=res, gate=gate, gate_idx=gate_idx,
                    rows_per_batch=rows_per_batch, colscale=None, rowscale=None, name=name)


def moe_hidden(a, w_in, combine, *, tm=512, tk=2048, name="moe_hidden"):
    M, K = a.shape
    E, _, F2 = w_in.shape
    F = F2 // 2
    tm, tk = _tile(M, tm), _tile(K, tk)
    specs = [pl.BlockSpec((None, tk, F), lambda i, j, k: (j, k, 0)),
             pl.BlockSpec((None, tk, F), lambda i, j, k: (j, k, 1))]
    return _mm_call(a, [w_in, w_in], specs, E * F, tm=tm, tn=F, nk=K // tk,
                    a_spec=pl.BlockSpec((tm, tk), lambda i, j, k: (i, k)),
                    act="swiglu", out_dtype=BF16, res=None, gate=None, gate_idx=0,
                    rows_per_batch=None, colscale=None, rowscale=combine, name=name)


def moe_out(hid, w_out, res, gate, gate_idx, rows_per_batch, *, tm=1024, tn=1024, name="moe_out"):
    M = hid.shape[0]
    E, F, D = w_out.shape
    tm, tn = _tile(rows_per_batch, tm), _tile(D, tn)
    return _mm_call(hid, [w_out], [pl.BlockSpec((None, F, tn), lambda i, j, k: (k, 0, j))], D,
                    tm=tm, tn=tn, nk=E, a_spec=pl.BlockSpec((tm, F), lambda i, j, k: (i, k)),
                    act=None, out_dtype=F32, res=res, gate=gate, gate_idx=gate_idx,
                    rows_per_batch=rows_per_batch, colscale=None, rowscale=None, name=name)


def pool_linear(a, w_pool, colscale, res, gate, gate_idx, rows_per_batch, *, tm=1024, tn=1024,
                name="pool_linear"):
    M, D = a.shape
    NG, Dg, _ = w_pool.shape
    tm, tn = _tile(rows_per_batch, tm), _tile(Dg, tn)
    per_g = Dg // tn
    return _mm_call(a, [w_pool],
                    [pl.BlockSpec((None, Dg, tn), lambda i, j, k: (j // per_g, 0, j % per_g))], D,
                    tm=tm, tn=tn, nk=1,
                    a_spec=pl.BlockSpec((tm, Dg), lambda i, j, k: (i, j // per_g)),
                    act=None, out_dtype=F32, res=res, gate=gate, gate_idx=gate_idx,
                    rows_per_batch=rows_per_batch, colscale=colscale, rowscale=None, name=name)


def _s5_tables(a_re, a_im, log_dt, b_re, b_im, c_re, c_im):
    C = S5_CHUNK
    G, P = a_re.shape[1], a_re.shape[2]

    def per_dir(dr):
        ar, ai = a_re[dr].astype(F32), a_im[dr].astype(F32)
        dt = jnp.exp(log_dt[dr].astype(F32))[:, None]
        mag = jnp.exp(ar * dt)
        abar_r, abar_i = mag * jnp.cos(ai * dt), mag * jnp.sin(ai * dt)
        den = ar * ar + ai * ai
        zr = ((abar_r - 1.0) * ar + abar_i * ai) / den
        zi = (abar_i * ar - (abar_r - 1.0) * ai) / den
        br, bi = b_re[dr].astype(F32), b_im[dr].astype(F32)
        bbar_r = zr[..., None] * br - zi[..., None] * bi
        bbar_i = zr[..., None] * bi + zi[..., None] * br

        def power(n):
            n = jnp.asarray(n, F32)[..., None, None]
            m = jnp.exp(n * (ar * dt))
            th = n * (ai * dt)
            return m * jnp.cos(th), m * jnp.sin(th)

        cr, ci = c_re[dr].astype(F32), c_im[dr].astype(F32)
        pr, pi_ = power(jnp.arange(C))
        cz_r = cr[None] * pr[:, :, None, :] - ci[None] * pi_[:, :, None, :]
        cz_i = cr[None] * pi_[:, :, None, :] + ci[None] * pr[:, :, None, :]
        kn = (jnp.einsum("ngip,gpj->gnij", cz_r, bbar_r, precision=HIGHEST)
              - jnp.einsum("ngip,gpj->gnij", cz_i, bbar_i, precision=HIGHEST))
        s_idx = jnp.arange(C)[:, None]
        t_idx = jnp.arange(C)[None, :]
        lag = (t_idx - s_idx) if dr == 0 else (s_idx - t_idx)
        valid = lag >= 0
        kt = kn[:, jnp.clip(lag, 0, C - 1)]
        kt = jnp.where(valid[None, :, :, None, None], kt, 0.0)
        tmat = kt.transpose(0, 1, 4, 2, 3).reshape(G, C * 16, C * 16)
        e = (C - 1 - jnp.arange(C)) if dr == 0 else jnp.arange(C)
        wr_, wi_ = power(e)
        w_re = wr_[..., None] * bbar_r[None] - wi_[..., None] * bbar_i[None]
        w_im = wr_[..., None] * bbar_i[None] + wi_[..., None] * bbar_r[None]
        wv = jnp.concatenate([w_re, w_im], axis=2)
        wv = wv.transpose(1, 0, 3, 2).reshape(G, C * 16, 2 * P)
        e = (jnp.arange(C) + 1) if dr == 0 else (C - jnp.arange(C))
        orr, oi = power(e)
        o_re = cr[None] * orr[:, :, None, :] - ci[None] * oi[:, :, None, :]
        o_im = -(cr[None] * oi[:, :, None, :] + ci[None] * orr[:, :, None, :])
        om = jnp.concatenate([o_re, o_im], axis=3)
        om = om.transpose(1, 3, 0, 2).reshape(G, 2 * P, C * 16)
        return tmat, wv, om

    t0, w0, o0 = per_dir(0)
    t1, w1, o1 = per_dir(1)
    tmat = t0 + t1
    wv = jnp.concatenate([w0, w1], axis=2)
    om = jnp.concatenate([o0, o1], axis=1)
    return tmat.astype(BF16), wv.astype(BF16), om.astype(BF16)


def _s5_scan_tables(a_re, a_im, log_dt, nlev):
    outs_r, outs_i = [], []
    for dr in range(2):
        ar, ai = a_re[dr].astype(F32), a_im[dr].astype(F32)
        dt = jnp.exp(log_dt[dr].astype(F32))[:, None]
        n = (S5_CHUNK * (2.0 ** jnp.arange(nlev, dtype=F32)))[:, None, None]
        m = jnp.exp(n * (ar * dt)[None])
        th = n * (ai * dt)[None]
        zr, zi = m * jnp.cos(th), m * jnp.sin(th)
        outs_r.append(jnp.concatenate([zr, zr], axis=-1))
        outs_i.append(jnp.concatenate([-zi, zi], axis=-1))
    zr = jnp.concatenate(outs_r, axis=-1).transpose(1, 0, 2)
    zi = jnp.concatenate(outs_i, axis=-1).transpose(1, 0, 2)
    return zr, zi


def _s5_kernel(x_ref, wv_ref, tm_ref, om_ref, zr_ref, zi_ref, y_ref, *, bsz, nc, nlev, p2):
    x = x_ref[...]
    v = jnp.dot(x, wv_ref[...], preferred_element_type=F32)
    row = lax.broadcasted_iota(jnp.int32, (nc, p2), 0)
    states = []
    for b in range(bsz):
        vb = v[b * nc:(b + 1) * nc]
        pf, pb = vb[:, :p2], vb[:, p2:]
        for k in range(nlev):
            d = 1 << k
            if d >= nc:
                break
            zr = zr_ref[k:k + 1, :]
            zi = zi_ref[k:k + 1, :]
            sf = jnp.where(row >= d, pltpu.roll(pf, d, 0), 0.0)
            pf = pf + zr[:, :p2] * sf + zi[:, :p2] * pltpu.roll(sf, p2 // 2, 1)
            sb = jnp.where(row < nc - d, pltpu.roll(pb, nc - d, 0), 0.0)
            pb = pb + zr[:, p2:] * sb + zi[:, p2:] * pltpu.roll(sb, p2 // 2, 1)
        sf = jnp.where(row >= 1, pltpu.roll(pf, 1, 0), 0.0)
        sb = jnp.where(row < nc - 1, pltpu.roll(pb, nc - 1, 0), 0.0)
        states.append(jnp.concatenate([sf, sb], axis=1))
    s = jnp.concatenate(states, axis=0).astype(BF16)
    y = jnp.dot(x, tm_ref[...], preferred_element_type=F32)
    y = y + jnp.dot(s, om_ref[...], preferred_element_type=F32)
    y_ref[...] = y


def s5_ssm(h, tables, scan_tables):
    bsz, L, D = h.shape
    C = S5_CHUNK
    G = D // S5_GROUP
    nc = L // C
    tmat, wv, om = tables
    zr, zi = scan_tables
    nlev = zr.shape[1]
    p2 = wv.shape[2] // 2
    cw = C * S5_GROUP
    xg = h.reshape(bsz, nc, C, G, S5_GROUP).transpose(3, 0, 1, 2, 4).reshape(G, bsz * nc, cw)
    y = pl.pallas_call(
        functools.partial(_s5_kernel, bsz=bsz, nc=nc, nlev=nlev, p2=p2),
        grid=(G,),
        in_specs=[pl.BlockSpec((None, bsz * nc, cw), lambda g: (g, 0, 0)),
                  pl.BlockSpec((None, cw, 2 * p2), lambda g: (g, 0, 0)),
                  pl.BlockSpec((None, cw, cw), lambda g: (g, 0, 0)),
                  pl.BlockSpec((None, 2 * p2, cw), lambda g: (g, 0, 0)),
                  pl.BlockSpec((None, nlev, 2 * p2), lambda g: (g, 0, 0)),
                  pl.BlockSpec((None, nlev, 2 * p2), lambda g: (g, 0, 0))],
        out_specs=pl.BlockSpec((None, bsz * nc, cw), lambda g: (g, 0, 0)),
        out_shape=jax.ShapeDtypeStruct((G, bsz * nc, cw), F32),
        compiler_params=_params(("parallel",)),
        name="s5_ssm",
    )(xg, wv, tmat, om, zr, zi)
    return y.reshape(G, bsz, nc, C, S5_GROUP).transpose(1, 2, 3, 0, 4).reshape(bsz, L, D)


def _gelu_skip_kernel(h_ref, y_ref, d_ref, o_ref):
    y = h_ref[...].astype(F32) * d_ref[...] + y_ref[...]
    o_ref[...] = jax.nn.gelu(y, approximate=True).astype(o_ref.dtype)


def gelu_skip(h, y, d_skip):
    bsz, L, D = h.shape
    tl = _tile(L, 512)
    spec = pl.BlockSpec((None, tl, D), lambda b, i: (b, i, 0))
    return pl.pallas_call(
        _gelu_skip_kernel, grid=(bsz, L // tl),
        in_specs=[spec, spec, pl.BlockSpec((1, D), lambda b, i: (0, 0))],
        out_specs=spec, out_shape=jax.ShapeDtypeStruct((bsz, L, D), BF16),
        compiler_params=_params(("parallel", "parallel")), name="gelu_skip",
    )(h, y, d_skip.reshape(1, D).astype(F32))


def _rel_bucket(rel):
    nb = REL_BUCKETS // 2
    max_exact = nb // 2
    n = jnp.abs(rel)
    large = max_exact + (jnp.log(jnp.maximum(n, 1).astype(F32) / max_exact)
                         / math.log(REL_MAX_DIST / max_exact) * (nb - max_exact)).astype(jnp.int32)
    large = jnp.minimum(large, nb - 1)
    return jnp.where(rel > 0, nb, 0) + jnp.where(n < max_exact, n, large)


def _bias_tiles_kernel(table_ref, bucket_ref, o_ref):
    h = pl.program_id(0)
    bk = bucket_ref[...]
    acc = jnp.zeros(bk.shape, F32)
    for b in range(REL_BUCKETS):
        acc = jnp.where(bk == b, table_ref[b, h], acc)
    o_ref[...] = acc * LOG2E


def bias_tiles(rel_bias, t):
    assert t >= REL_MAX_DIST
    H = rel_bias.shape[1]
    off = jnp.arange(-2, 3, dtype=jnp.int32)[:, None, None] * t
    rel = off + jnp.arange(t, dtype=jnp.int32)[None, None, :] - jnp.arange(t, dtype=jnp.int32)[None, :, None]
    buckets = _rel_bucket(rel)
    return pl.pallas_call(
        _bias_tiles_kernel, grid=(H, 5),
        in_specs=[pl.BlockSpec(memory_space=pltpu.SMEM),
                  pl.BlockSpec((None, t, t), lambda h, d: (d, 0, 0))],
        out_specs=pl.BlockSpec((None, None, t, t), lambda h, d: (h, d, 0, 0)),
        out_shape=jax.ShapeDtypeStruct((H, 5, t, t), F32),
        compiler_params=_params(("parallel", "parallel")), name="bias_tiles",
    )(rel_bias.astype(F32), buckets)


def _attn_kernel(q_ref, k_ref, v_ref, bias_ref, lam_ref, g_ref, o_ref, m_ref, l_ref, acc_ref, *,
                 nkv, lambda_init):
    j = pl.program_id(3)

    @pl.when(j == 0)
    def _():
        m_ref[...] = jnp.full(m_ref.shape, -jnp.inf, F32)
        l_ref[...] = jnp.zeros(l_ref.shape, F32)
        acc_ref[...] = jnp.zeros(acc_ref.shape, F32)

    dh = DA_HEAD_DIM
    tq, tk = q_ref.shape[0], k_ref.shape[0]
    rep = v_ref.shape[1] // LANES

    def rows_block(r, carry):
        rows = pl.ds(pl.multiple_of(r * ATTN_ROWS, ATTN_ROWS), ATTN_ROWS)
        q = q_ref[rows, :]
        bias = bias_ref[rows, :]
        v = v_ref[...]
        for m in range(2):
            s = lax.dot_general(q[:, m * dh:(m + 1) * dh], k_ref[:, m * dh:(m + 1) * dh],
                                (((1,), (1,)), ((), ())), preferred_element_type=F32) + bias
            m_prev = m_ref[m, rows, :]
            m_new = jnp.maximum(m_prev, jnp.max(s, axis=-1, keepdims=True))
            alpha = jnp.exp2(m_prev - m_new)
            p = jnp.exp2(s - pltpu.repeat(m_new, tk // LANES, axis=1))
            l_ref[m, rows, :] = alpha * l_ref[m, rows, :] + jnp.sum(p, axis=-1, keepdims=True)
            acc_ref[m, rows, :] = (pltpu.repeat(alpha, rep, axis=1) * acc_ref[m, rows, :]
                                   + jnp.dot(p.astype(v.dtype), v, preferred_element_type=F32))
            m_ref[m, rows, :] = m_new
        return carry

    lax.fori_loop(0, tq // ATTN_ROWS, rows_block, 0, unroll=ATTN_UNROLL)

    @pl.when(j == nkv - 1)
    def _():
        lam_p = lam_ref[...]
        lam = (jnp.exp(jnp.sum(lam_p[0:1] * lam_p[1:2], axis=-1, keepdims=True))
               - jnp.exp(jnp.sum(lam_p[2:3] * lam_p[3:4], axis=-1, keepdims=True)) + lambda_init)
        o = (acc_ref[0] / pltpu.repeat(l_ref[0], rep, axis=1)
             - lam * (acc_ref[1] / pltpu.repeat(l_ref[1], rep, axis=1)))
        o = o * lax.rsqrt(jnp.mean(o * o, axis=-1, keepdims=True) + EPS) * g_ref[...]
        o_ref[...] = (o * (1.0 - lambda_init)).astype(o_ref.dtype)


def diff_attention_core(qkv, bias, lam_params, subln_g, lambda_init, t):
    bsz, L, D3 = qkv.shape
    D = D3 // 3
    hw = 2 * DA_HEAD_DIM
    H = D // hw
    nq = L // t
    return pl.pallas_call(
        functools.partial(_attn_kernel, nkv=nq, lambda_init=lambda_init),
        grid=(bsz, H, nq, nq),
        in_specs=[pl.BlockSpec((None, t, hw), lambda b, h, i, j: (b, i, h)),
                  pl.BlockSpec((None, t, hw), lambda b, h, i, j: (b, j, H + h)),
                  pl.BlockSpec((None, t, hw), lambda b, h, i, j: (b, j, 2 * H + h)),
                  pl.BlockSpec((None, None, t, t),
                               lambda b, h, i, j: (h, jnp.clip(j - i, -2, 2) + 2, 0, 0)),
                  pl.BlockSpec((4, DA_HEAD_DIM), lambda b, h, i, j: (0, 0)),
                  pl.BlockSpec((1, hw), lambda b, h, i, j: (0, 0))],
        out_specs=pl.BlockSpec((None, t, hw), lambda b, h, i, j: (b, i, h)),
        out_shape=jax.ShapeDtypeStruct((bsz, L, D), BF16),
        scratch_shapes=[pltpu.VMEM((2, t, LANES), F32), pltpu.VMEM((2, t, LANES), F32),
                        pltpu.VMEM((2, t, hw), F32)],
        compiler_params=_params(("parallel", "parallel", "parallel", "arbitrary")),
        name="diff_attention",
    )(qkv, qkv, qkv, bias, lam_params, subln_g.reshape(1, hw).astype(F32))


def _hgrn_kernel(q_ref, f_ref, i_ref, gate_ref, lb_ref, ng_ref, tri_ref, o_ref, ofw_ref, s_ref, *,
                 tl, n_tiles, hb):
    dr = pl.program_id(2)
    t = pl.program_id(3)
    C = HG_CHUNK
    nch = tl // C
    W = hb * HG_DIM
    fwd = dr == 0

    @pl.when(t == 0)
    def _():
        s_ref[...] = jnp.zeros(s_ref.shape, F32)

    lb = lb_ref[...]
    sig = jax.nn.sigmoid(f_ref[...])
    logf = jnp.log(lb + (1.0 - lb) * sig)
    kk = (1.0 - lb) * (1.0 - sig)
    hi = logf.astype(BF16)
    lo = (logf - hi.astype(F32)).astype(BF16)
    tri = tri_ref[...]
    g = (jnp.dot(tri, hi, preferred_element_type=F32) + jnp.dot(tri, lo, preferred_element_type=F32))
    g3 = g.reshape(nch, C, W)
    g_tot = jnp.sum(logf.reshape(nch, C, W), axis=1, keepdims=True)
    qt = (q_ref[...] * jnp.exp(g)).astype(BF16)
    kt = (kk * jnp.exp(-g)).astype(BF16)
    kd = (kk.reshape(nch, C, W) * jnp.exp(g_tot - g3)).reshape(tl, W).astype(BF16)
    dec = jnp.exp(g_tot).reshape(nch, W)
    inp = i_ref[...].astype(BF16)
    row = lax.broadcasted_iota(jnp.int32, (C, C), 0)
    col = lax.broadcasted_iota(jnp.int32, (C, C), 1)
    causal = jnp.where(fwd, row - col, col - row) >= 0
    tile_idx = jnp.where(fwd, t, n_tiles - 1 - t)
    base = pl.multiple_of(tile_idx * tl, C)
    nt_dims = (((1,), (1,)), ((), ()))
    tn_dims = (((0,), (0,)), ((), ()))

    def run(order, is_fwd):
        for c in order:
            r = slice(c * C, (c + 1) * C)
            for hh in range(hb):
                cs = slice(hh * HG_DIM, (hh + 1) * HG_DIM)
                qc, kc, ic, kdc = qt[r, cs], kt[r, cs], inp[r, cs], kd[r, cs]
                att = lax.dot_general(qc, kc, nt_dims, preferred_element_type=F32)
                att = jnp.where(causal, att, 0.0).astype(BF16)
                st = s_ref[hh]
                o = (jnp.dot(att, ic, preferred_element_type=F32)
                     + lax.dot_general(qc, st.astype(BF16), nt_dims, preferred_element_type=F32))
                s_ref[hh] = (dec[c:c + 1, cs] * st
                             + lax.dot_general(ic, kdc, tn_dims, preferred_element_type=F32))
                rows = pl.ds(pl.multiple_of(base + c * C, C), C)
                if is_fwd:
                    ofw_ref[rows, cs] = o
                else:
                    tot = ofw_ref[rows, cs] + o
                    y = tot * lax.rsqrt(jnp.mean(tot * tot, axis=-1, keepdims=True) + EPS) * ng_ref[...]
                    gt = gate_ref[r, cs]
                    o_ref[r, cs] = (y * (gt * jax.nn.sigmoid(gt))).astype(o_ref.dtype)

    @pl.when(fwd)
    def _():
        run(range(nch), True)

    @pl.when(jnp.logical_not(fwd))
    def _():
        run(range(nch - 1, -1, -1), False)


def hgrn2_core(proj, lb, norm_g, *, tl=512, hb=HG_HEADS_PER_STEP):
    bsz, L, D5 = proj.shape
    D = D5 // 5
    H = D // HG_DIM
    hb = min(hb, H)
    HB = H // hb
    W = hb * HG_DIM
    tl = _tile(L, tl)
    nt = L // tl
    C = HG_CHUNK
    idx = jnp.arange(tl, dtype=jnp.int32)
    same = (idx[:, None] // C) == (idx[None, :] // C)
    tri = jnp.stack([same & (idx[None, :] <= idx[:, None]),
                     same & (idx[None, :] >= idx[:, None])]).astype(BF16)

    def tile_of(dr, t):
        return jnp.where(dr == 0, t, nt - 1 - t)

    def in_spec(section):
        return pl.BlockSpec((None, tl, W), lambda b, h, dr, t: (b, tile_of(dr, t), section * HB + h))

    f_spec = pl.BlockSpec((None, tl, W), lambda b, h, dr, t: (b, tile_of(dr, t), (1 + dr) * HB + h))
    out_spec = pl.BlockSpec((None, tl, W),
                            lambda b, h, dr, t: (b, jnp.where(dr == 0, nt - 1, nt - 1 - t), h))
    return pl.pallas_call(
        functools.partial(_hgrn_kernel, tl=tl, n_tiles=nt, hb=hb),
        grid=(bsz, HB, 2, nt),
        in_specs=[in_spec(0), f_spec, in_spec(3), in_spec(4),
                  pl.BlockSpec((1, W), lambda b, h, dr, t: (0, h)),
                  pl.BlockSpec((1, HG_DIM), lambda b, h, dr, t: (0, 0)),
                  pl.BlockSpec((None, tl, tl), lambda b, h, dr, t: (dr, 0, 0))],
        out_specs=out_spec,
        out_shape=jax.ShapeDtypeStruct((bsz, L, D), BF16),
        scratch_shapes=[pltpu.VMEM((L, W), F32), pltpu.VMEM((hb, HG_DIM, HG_DIM), F32)],
        compiler_params=_params(("parallel", "parallel", "arbitrary", "arbitrary")),
        name="hgrn2",
    )(proj, proj, proj, proj, lb.reshape(1, D).astype(F32), norm_g.reshape(1, HG_DIM).astype(F32), tri)


def _pool_kernel(h_ref, o_ref, pad_ref, *, L):
    grp = pl.program_id(1)
    x = h_ref[...].astype(F32)
    zeros = jnp.zeros((POOL_HALO, x.shape[1]), F32)
    pad_ref[0:POOL_HALO, :] = zeros
    pad_ref[POOL_HALO:POOL_HALO + L, :] = x
    pad_ref[POOL_HALO + L:POOL_HALO + L + POOL_HALO, :] = zeros
    pos = lax.broadcasted_iota(jnp.int32, (L, 1), 0)
    for gi, w in enumerate(POOL_WINDOWS):
        @pl.when(grp == gi)
        def _(w=w):
            acc = jnp.zeros(x.shape, F32)
            for d in range(-(w // 2), w - w // 2):
                acc = acc + pad_ref[POOL_HALO + d:POOL_HALO + d + L, :]
            lo = jnp.clip(pos - w // 2, 0, L)
            hi = jnp.clip(pos + w - w // 2, 0, L)
            cnt = (hi - lo).astype(F32)
            o_ref[...] = (acc / cnt - x).astype(o_ref.dtype)


def pool_core(h, *, width=128):
    bsz, L, D = h.shape
    ng = len(POOL_WINDOWS)
    dg = D // ng
    width = _tile(dg, width)
    per_g = dg // width
    spec = pl.BlockSpec((None, L, width), lambda b, g, s: (b, 0, g * per_g + s))
    return pl.pallas_call(
        functools.partial(_pool_kernel, L=L),
        grid=(bsz, ng, per_g),
        in_specs=[spec], out_specs=spec,
        out_shape=jax.ShapeDtypeStruct((bsz, L, D), BF16),
        scratch_shapes=[pltpu.VMEM((L + 2 * POOL_HALO, width), F32)],
        compiler_params=_params(("parallel", "parallel", "parallel")),
        name="pool",
    )(h)


def _trunk(x, c, p, shared):
    bsz, L, D = x.shape
    T = bsz * L
    depth = p["norm1_g"].shape[0]
    c_low = small_matmul(c, p["w_ada_down"], silu_in=True)
    for l in range(depth):
        mod = small_matmul(c_low, p["w_ada"][l], p["b_ada"][l]).reshape(bsz, 6, 1, D)
        h = norm_mod(x, p["norm1_g"][l], mod, 1, 0)
        kind, j = l % 4, l // 4
        x2 = x.reshape(T, D)
        if kind == 0:
            y = s5_ssm(h, shared["s5_tables"][j], shared["s5_scan"][(j, L)])
            z = gelu_skip(h, y, p["s5_d"][j])
            x = glu_linear(z.reshape(T, D), shared["s5_w_glu"][j], "sigglu", out_dtype=F32,
                           res=x2, gate=mod, gate_idx=2, rows_per_batch=L, name="s5_glu")
        elif kind == 1:
            lambda_init = 0.8 - 0.6 * math.exp(-0.3 * l)
            qkv = linear(h.reshape(T, D), shared["da_w_qkv"][j], name="da_qkv")
            lam_params = jnp.stack([p["da_lam_q1"][j], p["da_lam_k1"][j],
                                    p["da_lam_q2"][j], p["da_lam_k2"][j]]).astype(F32)
            o = diff_attention_core(qkv.reshape(bsz, L, 3 * D), shared["bias_tiles"], lam_params,
                                    p["da_subln_g"][j], lambda_init, shared["attn_tile"])
            x = linear(o.reshape(T, D), shared["da_w_o"][j], out_dtype=F32, res=x2, gate=mod,
                       gate_idx=2, rows_per_batch=L, name="da_out")
        elif kind == 2:
            proj = linear(h.reshape(T, D), shared["hg_w_in"][j], out_dtype=F32, name="hg_in")
            o = hgrn2_core(proj.reshape(bsz, L, 5 * D), shared["hg_lb"][l], p["hg_norm_g"][j])
            x = linear(o.reshape(T, D), shared["hg_w_o"][j], out_dtype=F32, res=x2, gate=mod,
                       gate_idx=2, rows_per_batch=L, name="hg_out")
        else:
            pooled = pool_core(h)
            x = pool_linear(pooled.reshape(T, D), shared["pool_w"][j], p["pool_scale"][j].astype(F32),
                            x2, mod, 2, L)
        x = x.reshape(bsz, L, D)
        x2 = x.reshape(T, D)
        if l % 2 == 0:
            h = norm_mod(x, p["norm2_g"][l], mod, 4, 3)
            hid = glu_linear(h.reshape(T, D), shared["ff_w_in"][l // 2], "swiglu", name="ff_in")
            x = linear(hid, shared["ff_w_out"][l // 2], tk=_ff_tk(hid.shape[1]), out_dtype=F32,
                       res=x2, gate=mod, gate_idx=5, rows_per_batch=L, name="ff_out")
        else:
            h, comb = norm_mod(x, p["norm2_g"][l], mod, 4, 3, w_router=p["moe_router"][l // 2])
            hid = moe_hidden(h.reshape(T, D), shared["moe_w_in"][l // 2], comb.reshape(T, LANES))
            x = moe_out(hid, shared["moe_w_out"][l // 2], x2, mod, 5, L)
        x = x.reshape(bsz, L, D)
    return final_norm(x, p["final_g"])


def _ff_tk(k):
    return k // 2 if (k // 2) % LANES == 0 else k


def kernel(x_prompt, x_sample, c_prompt, c_sample, norm1_g, norm2_g, final_g, w_ada_down, w_ada, b_ada,
           s5_a_re, s5_a_im, s5_log_dt, s5_b_re, s5_b_im, s5_c_re, s5_c_im, s5_d, s5_w_glu,
           da_w_qkv, da_w_o, da_lam_q1, da_lam_k1, da_lam_q2, da_lam_k2, da_subln_g, rel_bias,
           hg_w_in, hg_w_o, hg_norm_g, hg_lb_logits,
           pool_w, pool_scale,
           ff_w_in, ff_w_out, moe_router, moe_w_in, moe_w_out):
    p = dict(norm1_g=norm1_g, norm2_g=norm2_g, final_g=final_g, w_ada_down=w_ada_down, w_ada=w_ada,
             b_ada=b_ada, s5_d=s5_d, da_lam_q1=da_lam_q1, da_lam_k1=da_lam_k1, da_lam_q2=da_lam_q2,
             da_lam_k2=da_lam_k2, da_subln_g=da_subln_g, hg_norm_g=hg_norm_g, pool_scale=pool_scale,
             moe_router=moe_router)
    seq_lens = sorted({x_prompt.shape[1], x_sample.shape[1]})
    attn_tile = _tile(min(seq_lens), 512)
    lb_cum = jnp.cumsum(jax.nn.softmax(hg_lb_logits.astype(F32), axis=0), axis=0)
    hg_lb = jnp.concatenate([jnp.zeros_like(lb_cum[:1]), lb_cum[:-1]], axis=0)
    s5_scan = {}
    for j in range(s5_a_re.shape[0]):
        for L in seq_lens:
            nlev = max(1, math.ceil(math.log2(L // S5_CHUNK)))
            s5_scan[(j, L)] = _s5_scan_tables(s5_a_re[j], s5_a_im[j], s5_log_dt[j], nlev)
    d_model = x_prompt.shape[-1]
    qkv_scale = jnp.concatenate([jnp.full((d_model,), DA_HEAD_DIM ** -0.5 * LOG2E, F32),
                                 jnp.ones((2 * d_model,), F32)])
    shared = dict(
        s5_tables=[_s5_tables(s5_a_re[j], s5_a_im[j], s5_log_dt[j], s5_b_re[j], s5_b_im[j],
                              s5_c_re[j], s5_c_im[j]) for j in range(s5_a_re.shape[0])],
        s5_scan=s5_scan,
        s5_w_glu=s5_w_glu.astype(BF16), da_w_qkv=(da_w_qkv * qkv_scale).astype(BF16),
        da_w_o=da_w_o.astype(BF16),
        hg_w_in=hg_w_in.astype(BF16), hg_w_o=hg_w_o.astype(BF16), pool_w=pool_w.astype(BF16),
        ff_w_in=ff_w_in.astype(BF16), ff_w_out=ff_w_out.astype(BF16),
        moe_w_in=moe_w_in.astype(BF16), moe_w_out=moe_w_out.astype(BF16),
        hg_lb=hg_lb, attn_tile=attn_tile, bias_tiles=bias_tiles(rel_bias, attn_tile),
    )
    y_prompt = _trunk(x_prompt, c_prompt, p, shared)
    y_sample = _trunk(x_sample, c_sample, p, shared)
    return (y_prompt, y_sample)
```

```python
import functools
import math

import jax
import jax.numpy as jnp
from jax import lax
from jax.experimental import pallas as pl
from jax.experimental.pallas import tpu as pltpu

F32 = jnp.float32
BF16 = jnp.bfloat16
EPS = 1e-6
LANES = 128
V7X_VMEM_LIMIT = 60 * 1024 * 1024

S5_GROUP = 16
S5_CHUNK = 16
DA_HEAD_DIM = 128
ATTN_ROWS = 128
ATTN_UNROLL = 4
REL_BUCKETS = 32
REL_MAX_DIST = 128
HG_DIM = 128
HG_CHUNK = 64
HG_HEADS_PER_STEP = 4
POOL_WINDOWS = (2, 4, 8, 16)
POOL_HALO = 8
MOE_ROW_TILE = 512
HIGHEST = lax.Precision.HIGHEST
LOG2E = math.log2(math.e)


def _params(sem, vmem=V7X_VMEM_LIMIT):
    return pltpu.CompilerParams(dimension_semantics=sem, vmem_limit_bytes=vmem)


def _tile(n, pref):
    t = min(n, pref)
    while n % t:
        t //= 2
    return t


def _small_mm_kernel(a_ref, w_ref, b_ref, o_ref, *, silu_in):
    a = a_ref[...]
    if silu_in:
        a = a * jax.nn.sigmoid(a)
    o_ref[...] = jnp.dot(a, w_ref[...], preferred_element_type=F32, precision=HIGHEST) + b_ref[...]


def small_matmul(a, w, b=None, *, silu_in=False):
    m, k = a.shape
    n = w.shape[1]
    tn = _tile(n, 2048)
    if b is None:
        b = jnp.zeros((1, n), F32)
    return pl.pallas_call(
        functools.partial(_small_mm_kernel, silu_in=silu_in),
        grid=(n // tn,),
        in_specs=[pl.BlockSpec((m, k), lambda j: (0, 0)),
                  pl.BlockSpec((k, tn), lambda j: (0, j)),
                  pl.BlockSpec((1, tn), lambda j: (0, j))],
        out_specs=pl.BlockSpec((m, tn), lambda j: (0, j)),
        out_shape=jax.ShapeDtypeStruct((m, n), F32),
        compiler_params=_params(("parallel",)),
        name="small_matmul",
    )(a, w, b.reshape(1, n))


def _norm_mod_kernel(x_ref, g_ref, sc_ref, sh_ref, o_ref):
    x = x_ref[...]
    y = x * lax.rsqrt(jnp.mean(x * x, axis=-1, keepdims=True) + EPS) * g_ref[...]
    o_ref[...] = (y * (1.0 + sc_ref[...]) + sh_ref[...]).astype(o_ref.dtype)


ROUTE_E1, ROUTE_E2, ROUTE_R1, ROUTE_R2, ROUTE_G1, ROUTE_G2 = range(6)


def _norm_mod_router_kernel(x_ref, g_ref, sc_ref, sh_ref, wr_ref, tri_ref, o_ref, route_ref, cnt_ref,
                            run_ref, *, n_experts):
    @pl.when((pl.program_id(0) == 0) & (pl.program_id(1) == 0))
    def _():
        run_ref[...] = jnp.zeros(run_ref.shape, F32)

    x = x_ref[...]
    y = x * lax.rsqrt(jnp.mean(x * x, axis=-1, keepdims=True) + EPS) * g_ref[...]
    h = y * (1.0 + sc_ref[...]) + sh_ref[...]
    o_ref[...] = h.astype(o_ref.dtype)
    logits = jnp.dot(h, wr_ref[...], preferred_element_type=F32, precision=HIGHEST)
    lane = lax.broadcasted_iota(jnp.int32, logits.shape, 1)
    neg = jnp.float32(-jnp.inf)
    logits = jnp.where(lane < n_experts, logits, neg)
    v1 = jnp.max(logits, axis=-1, keepdims=True)
    i1 = jnp.min(jnp.where(logits == v1, lane, LANES), axis=-1, keepdims=True)
    rest = jnp.where(lane == i1, neg, logits)
    v2 = jnp.max(rest, axis=-1, keepdims=True)
    i2 = jnp.min(jnp.where(rest == v2, lane, LANES), axis=-1, keepdims=True)
    e2 = jnp.exp(v2 - v1)
    g1 = 1.0 / (1.0 + e2)
    g2 = e2 / (1.0 + e2)
    sel = jnp.where((lane == i1) | (lane == i2), 1.0, 0.0)
    before = run_ref[...] + jnp.dot(tri_ref[...], sel.astype(BF16), preferred_element_type=F32)
    r1 = jnp.sum(jnp.where(lane == i1, before, 0.0), axis=-1, keepdims=True)
    r2 = jnp.sum(jnp.where(lane == i2, before, 0.0), axis=-1, keepdims=True)
    run_ref[...] = run_ref[...] + jnp.sum(sel, axis=0, keepdims=True)
    cnt_ref[...] = run_ref[...]
    rec = jnp.zeros(logits.shape, F32)
    for k, val in ((ROUTE_E1, i1.astype(F32)), (ROUTE_E2, i2.astype(F32)), (ROUTE_R1, r1),
                   (ROUTE_R2, r2), (ROUTE_G1, g1), (ROUTE_G2, g2)):
        rec = jnp.where(lane == k, val, rec)
    route_ref[...] = rec


def norm_mod(x, g, mod, sc_idx, sh_idx, w_router=None):
    bsz, L, D = x.shape
    tl = _tile(L, 256)
    grid = (bsz, L // tl)
    x_spec = pl.BlockSpec((None, tl, D), lambda b, i: (b, i, 0))
    g_spec = pl.BlockSpec((1, D), lambda b, i: (0, 0))
    sc_spec = pl.BlockSpec((None, None, 1, D), lambda b, i: (b, sc_idx, 0, 0))
    sh_spec = pl.BlockSpec((None, None, 1, D), lambda b, i: (b, sh_idx, 0, 0))
    h_shape = jax.ShapeDtypeStruct((bsz, L, D), BF16)
    if w_router is None:
        return pl.pallas_call(
            _norm_mod_kernel, grid=grid,
            in_specs=[x_spec, g_spec, sc_spec, sh_spec],
            out_specs=x_spec, out_shape=h_shape,
            compiler_params=_params(("parallel", "parallel")), name="norm_mod",
        )(x, g.reshape(1, D), mod, mod)
    n_experts = w_router.shape[1]
    wr = jnp.zeros((D, LANES), F32).at[:, :n_experts].set(w_router)
    idx = jnp.arange(tl, dtype=jnp.int32)
    tri = (idx[None, :] < idx[:, None]).astype(BF16)
    return pl.pallas_call(
        functools.partial(_norm_mod_router_kernel, n_experts=n_experts), grid=grid,
        in_specs=[x_spec, g_spec, sc_spec, sh_spec, pl.BlockSpec((D, LANES), lambda b, i: (0, 0)),
                  pl.BlockSpec((tl, tl), lambda b, i: (0, 0))],
        out_specs=[x_spec, pl.BlockSpec((None, tl, LANES), lambda b, i: (b, i, 0)),
                   pl.BlockSpec((1, LANES), lambda b, i: (0, 0))],
        out_shape=[jax.ShapeDtypeStruct((bsz, L, D), F32), jax.ShapeDtypeStruct((bsz, L, LANES), F32),
                   jax.ShapeDtypeStruct((1, LANES), F32)],
        scratch_shapes=[pltpu.VMEM((1, LANES), F32)],
        compiler_params=_params(("arbitrary", "arbitrary")), name="norm_mod_router",
    )(x, g.reshape(1, D), mod, mod, wr, tri)


def _final_norm_kernel(x_ref, g_ref, o_ref):
    x = x_ref[...]
    o_ref[...] = x * lax.rsqrt(jnp.mean(x * x, axis=-1, keepdims=True) + EPS) * g_ref[...]


def final_norm(x, g):
    bsz, L, D = x.shape
    tl = _tile(L, 256)
    spec = pl.BlockSpec((None, tl, D), lambda b, i: (b, i, 0))
    return pl.pallas_call(
        _final_norm_kernel, grid=(bsz, L // tl),
        in_specs=[spec, pl.BlockSpec((1, D), lambda b, i: (0, 0))],
        out_specs=spec, out_shape=jax.ShapeDtypeStruct((bsz, L, D), F32),
        compiler_params=_params(("parallel", "parallel")), name="final_norm",
    )(x, g.reshape(1, D))


def _mm_kernel(*refs, act, nk, has_res, has_colscale):
    it = iter(refs)
    a_ref = next(it)
    w1_ref = next(it)
    w2_ref = next(it) if act else None
    res_ref = next(it) if has_res else None
    gate_ref = next(it) if has_res else None
    cs_ref = next(it) if has_colscale else None
    o_ref = next(it)
    acc1_ref = next(it) if nk > 1 else None
    acc2_ref = next(it) if (nk > 1 and act) else None
    k = pl.program_id(2)

    a = a_ref[...]
    p1 = jnp.dot(a, w1_ref[...], preferred_element_type=F32)
    p2 = jnp.dot(a, w2_ref[...], preferred_element_type=F32) if act else None

    def finish(y1, y2):
        if act == "swiglu":
            y = y1 * jax.nn.sigmoid(y1) * y2
        elif act == "sigglu":
            y = y1 * jax.nn.sigmoid(y2)
        else:
            y = y1
        if has_colscale:
            y = y * cs_ref[...]
        if has_res:
            y = res_ref[...] + gate_ref[...] * y
        o_ref[...] = y.astype(o_ref.dtype)

    if nk == 1:
        finish(p1, p2)
    else:
        @pl.when(k == 0)
        def _():
            acc1_ref[...] = p1
            if act:
                acc2_ref[...] = p2

        @pl.when(k > 0)
        def _():
            acc1_ref[...] += p1
            if act:
                acc2_ref[...] += p2

        @pl.when(k == nk - 1)
        def _():
            finish(acc1_ref[...], acc2_ref[...] if act else None)


def _mm_call(a, weights, w_specs, n_out, *, tm, tn, nk, a_spec, act, out_dtype,
             res, gate, gate_idx, rows_per_batch, colscale, name):
    M = a.shape[0]
    grid = (M // tm, n_out // tn, nk)
    in_specs = [a_spec] + list(w_specs)
    args = [a] + list(weights)
    if res is not None:
        tiles_per_batch = rows_per_batch // tm
        in_specs.append(pl.BlockSpec((tm, tn), lambda i, j, k: (i, j)))
        in_specs.append(pl.BlockSpec((None, None, 1, tn),
                                     lambda i, j, k: (i // tiles_per_batch, gate_idx, 0, j)))
        args += [res, gate]
    if colscale is not None:
        in_specs.append(pl.BlockSpec((1, tn), lambda i, j, k: (0, j)))
        args.append(colscale.reshape(1, n_out))
    scratch = []
    if nk > 1:
        scratch.append(pltpu.VMEM((tm, tn), F32))
        if act:
            scratch.append(pltpu.VMEM((tm, tn), F32))
    return pl.pallas_call(
        functools.partial(_mm_kernel, act=act, nk=nk, has_res=res is not None,
                          has_colscale=colscale is not None),
        grid=grid, in_specs=in_specs,
        out_specs=pl.BlockSpec((tm, tn), lambda i, j, k: (i, j)),
        out_shape=jax.ShapeDtypeStruct((M, n_out), out_dtype),
        scratch_shapes=scratch,
        compiler_params=_params(("parallel", "parallel", "arbitrary")),
        name=name,
    )(*args)


def linear(a, w, *, tm=1024, tn=1024, tk=None, out_dtype=BF16, res=None, gate=None, gate_idx=0,
           rows_per_batch=None, name="linear"):
    M, K = a.shape
    N = w.shape[1]
    tm, tn = _tile(M if rows_per_batch is None else rows_per_batch, tm), _tile(N, tn)
    tk = K if tk is None else tk
    nk = K // tk
    return _mm_call(a, [w], [pl.BlockSpec((tk, tn), lambda i, j, k: (k, j))], N,
                    tm=tm, tn=tn, nk=nk, a_spec=pl.BlockSpec((tm, tk), lambda i, j, k: (i, k)),
                    act=None, out_dtype=out_dtype, res=res, gate=gate, gate_idx=gate_idx,
                    rows_per_batch=rows_per_batch, colscale=None, name=name)


def glu_linear(a, w, act, *, tm=1024, tn=512, out_dtype=BF16, res=None, gate=None, gate_idx=0,
               rows_per_batch=None, name="glu_linear"):
    M, K = a.shape
    F = w.shape[1] // 2
    tm, tn = _tile(M if rows_per_batch is None else rows_per_batch, tm), _tile(F, tn)
    off = F // tn
    specs = [pl.BlockSpec((K, tn), lambda i, j, k: (0, j)),
             pl.BlockSpec((K, tn), lambda i, j, k: (0, j + off))]
    return _mm_call(a, [w, w], specs, F, tm=tm, tn=tn, nk=1,
                    a_spec=pl.BlockSpec((tm, K), lambda i, j, k: (i, 0)),
                    act=act, out_dtype=out_dtype, res=res, gate=gate, gate_idx=gate_idx,
                    rows_per_batch=rows_per_batch, colscale=None, name=name)


def moe_plan(route, counts, n_experts, tm):
    T = route.shape[0]
    e1, e2 = route[:, ROUTE_E1].astype(jnp.int32), route[:, ROUTE_E2].astype(jnp.int32)
    r1, r2 = route[:, ROUTE_R1].astype(jnp.int32), route[:, ROUTE_R2].astype(jnp.int32)
    cnt = counts[0, :n_experts].astype(jnp.int32)
    gsz = (cnt + tm - 1) // tm * tm
    ends = jnp.cumsum(gsz)
    offs = ends - gsz

    def start_of(e):
        return sum(jnp.where(e == k, offs[k], 0) for k in range(n_experts))

    n_tiles = (2 * T) // tm + n_experts
    starts = jnp.arange(n_tiles, dtype=jnp.int32) * tm
    tile_e = jnp.minimum(jnp.sum((starts[:, None] >= ends[None, :]).astype(jnp.int32), axis=1),
                         n_experts - 1)
    n_valid = (ends[-1] // tm).reshape(1)
    return start_of(e1) + r1, start_of(e2) + r2, tile_e, n_valid, n_tiles


def _row_copies(src_at, dst_at, sem, n, start):
    def body(r, carry):
        for cp in (pltpu.make_async_copy(s_, d_, sem) for s_, d_ in zip(src_at(r), dst_at(r))):
            cp.start() if start else cp.wait()
        return carry
    lax.fori_loop(0, n, body, 0)


def _moe_dispatch_kernel(d1_ref, d2_ref, h_ref, xs_zero_ref, xs_ref, sem, *, tl):
    del xs_zero_ref
    base = pl.program_id(0) * tl

    def src_at(r):
        row = h_ref.at[pl.ds(base + r, 1), :]
        return row, row

    def dst_at(r):
        return xs_ref.at[pl.ds(d1_ref[base + r], 1), :], xs_ref.at[pl.ds(d2_ref[base + r], 1), :]

    _row_copies(src_at, dst_at, sem, tl, True)
    _row_copies(src_at, dst_at, sem, tl, False)


def moe_dispatch(h, d1, d2, n_rows, *, tl=256):
    T, D = h.shape
    tl = _tile(T, tl)
    any_spec = pl.BlockSpec(memory_space=pl.ANY)
    return pl.pallas_call(
        functools.partial(_moe_dispatch_kernel, tl=tl),
        grid_spec=pltpu.PrefetchScalarGridSpec(
            num_scalar_prefetch=2, grid=(T // tl,), in_specs=[any_spec, any_spec], out_specs=any_spec,
            scratch_shapes=[pltpu.SemaphoreType.DMA(())]),
        out_shape=jax.ShapeDtypeStruct((n_rows, D), F32),
        input_output_aliases={3: 0},
        compiler_params=_params(("arbitrary",)), name="moe_dispatch",
    )(d1, d2, h, jnp.zeros((n_rows, D), F32))


def _moe_up_kernel(te_ref, nv_ref, a_ref, wg_ref, wu_ref, o_ref, accg_ref, accu_ref, *, nk):
    del te_ref
    k = pl.program_id(1)
    valid = pl.program_id(0) < nv_ref[0]

    @pl.when(valid)
    def _():
        a = a_ref[...].astype(BF16)
        pg = jnp.dot(a, wg_ref[...], preferred_element_type=F32)
        pu = jnp.dot(a, wu_ref[...], preferred_element_type=F32)

        @pl.when(k == 0)
        def _():
            accg_ref[...] = pg
            accu_ref[...] = pu

        @pl.when(k > 0)
        def _():
            accg_ref[...] += pg
            accu_ref[...] += pu

        @pl.when(k == nk - 1)
        def _():
            y = accg_ref[...]
            o_ref[...] = (y * jax.nn.sigmoid(y) * accu_ref[...]).astype(o_ref.dtype)

    @pl.when(jnp.logical_not(valid) & (k == nk - 1))
    def _():
        o_ref[...] = jnp.zeros(o_ref.shape, o_ref.dtype)


def moe_up(xs, w_in, tile_e, n_valid, *, tm, tk=2048):
    R, K = xs.shape
    F = w_in.shape[2] // 2
    tk = _tile(K, tk)
    nk = K // tk
    return pl.pallas_call(
        functools.partial(_moe_up_kernel, nk=nk),
        grid_spec=pltpu.PrefetchScalarGridSpec(
            num_scalar_prefetch=2, grid=(R // tm, nk),
            in_specs=[pl.BlockSpec((tm, tk), lambda i, k, te, nv: (i, k)),
                      pl.BlockSpec((None, tk, F), lambda i, k, te, nv: (te[i], k, 0)),
                      pl.BlockSpec((None, tk, F), lambda i, k, te, nv: (te[i], k, 1))],
            out_specs=pl.BlockSpec((tm, F), lambda i, k, te, nv: (i, 0)),
            scratch_shapes=[pltpu.VMEM((tm, F), F32), pltpu.VMEM((tm, F), F32)]),
        out_shape=jax.ShapeDtypeStruct((R, F), BF16),
        compiler_params=_params(("parallel", "arbitrary")), name="moe_up",
    )(tile_e, n_valid, xs, w_in, w_in)


def _moe_down_kernel(te_ref, nv_ref, a_ref, w_ref, o_ref):
    del te_ref
    valid = pl.program_id(0) < nv_ref[0]

    @pl.when(valid)
    def _():
        o_ref[...] = jnp.dot(a_ref[...], w_ref[...], preferred_element_type=F32)

    @pl.when(jnp.logical_not(valid))
    def _():
        o_ref[...] = jnp.zeros(o_ref.shape, o_ref.dtype)


def moe_down(hs, w_out, tile_e, n_valid, *, tm, tn=1024):
    R, F = hs.shape
    D = w_out.shape[2]
    tn = _tile(D, tn)
    return pl.pallas_call(
        _moe_down_kernel,
        grid_spec=pltpu.PrefetchScalarGridSpec(
            num_scalar_prefetch=2, grid=(R // tm, D // tn),
            in_specs=[pl.BlockSpec((tm, F), lambda i, j, te, nv: (i, 0)),
                      pl.BlockSpec((None, F, tn), lambda i, j, te, nv: (te[i], 0, j))],
            out_specs=pl.BlockSpec((tm, tn), lambda i, j, te, nv: (i, j))),
        out_shape=jax.ShapeDtypeStruct((R, D), F32),
        compiler_params=_params(("parallel", "parallel")), name="moe_down",
    )(tile_e, n_valid, hs, w_out)


def _moe_combine_kernel(d1_ref, d2_ref, ys_ref, route_ref, res_ref, gate_ref, o_ref, buf_ref, sem, *, tl):
    base = pl.program_id(0) * tl

    def src_at(r):
        return ys_ref.at[pl.ds(d1_ref[base + r], 1), :], ys_ref.at[pl.ds(d2_ref[base + r], 1), :]

    def dst_at(r):
        return buf_ref.at[0, pl.ds(r, 1), :], buf_ref.at[1, pl.ds(r, 1), :]

    _row_copies(src_at, dst_at, sem, tl, True)
    _row_copies(src_at, dst_at, sem, tl, False)
    route = route_ref[...]
    g1 = route[:, ROUTE_G1:ROUTE_G1 + 1]
    g2 = route[:, ROUTE_G2:ROUTE_G2 + 1]
    o_ref[...] = res_ref[...] + gate_ref[...] * (g1 * buf_ref[0] + g2 * buf_ref[1])


def moe_combine(ys, route, d1, d2, res, gate, gate_idx, rows_per_batch, *, tl=256):
    T, D = res.shape
    tl = _tile(rows_per_batch, tl)
    per_b = rows_per_batch // tl
    return pl.pallas_call(
        functools.partial(_moe_combine_kernel, tl=tl),
        grid_spec=pltpu.PrefetchScalarGridSpec(
            num_scalar_prefetch=2, grid=(T // tl,),
            in_specs=[pl.BlockSpec(memory_space=pl.ANY),
                      pl.BlockSpec((tl, LANES), lambda i, d1, d2: (i, 0)),
                      pl.BlockSpec((tl, D), lambda i, d1, d2: (i, 0)),
                      pl.BlockSpec((None, None, 1, D), lambda i, d1, d2: (i // per_b, gate_idx, 0, 0))],
            out_specs=pl.BlockSpec((tl, D), lambda i, d1, d2: (i, 0)),
            scratch_shapes=[pltpu.VMEM((2, tl, D), F32), pltpu.SemaphoreType.DMA(())]),
        out_shape=jax.ShapeDtypeStruct((T, D), F32),
        compiler_params=_params(("arbitrary",)), name="moe_combine",
    )(d1, d2, ys, route, res, gate)


def moe_ffn(h, route, counts, w_in, w_out, res, gate, gate_idx, rows_per_batch):
    T = h.shape[0]
    n_experts = w_in.shape[0]
    tm = _tile(T, MOE_ROW_TILE)
    d1, d2, tile_e, n_valid, n_tiles = moe_plan(route, counts, n_experts, tm)
    xs = moe_dispatch(h, d1, d2, n_tiles * tm)
    hs = moe_up(xs, w_in, tile_e, n_valid, tm=tm)
    ys = moe_down(hs, w_out, tile_e, n_valid, tm=tm)
    return moe_combine(ys, route, d1, d2, res, gate, gate_idx, rows_per_batch)


def pool_linear(a, w_pool, colscale, res, gate, gate_idx, rows_per_batch, *, tm=1024, tn=1024,
                name="pool_linear"):
    M, D = a.shape
    NG, Dg, _ = w_pool.shape
    tm, tn = _tile(rows_per_batch, tm), _tile(Dg, tn)
    per_g = Dg // tn
    return _mm_call(a, [w_pool],
                    [pl.BlockSpec((None, Dg, tn), lambda i, j, k: (j // per_g, 0, j % per_g))], D,
                    tm=tm, tn=tn, nk=1,
                    a_spec=pl.BlockSpec((tm, Dg), lambda i, j, k: (i, j // per_g)),
                    act=None, out_dtype=F32, res=res, gate=gate, gate_idx=gate_idx,
                    rows_per_batch=rows_per_batch, colscale=colscale, name=name)


def _s5_tables(a_re, a_im, log_dt, b_re, b_im, c_re, c_im):
    C = S5_CHUNK
    G, P = a_re.shape[1], a_re.shape[2]

    def per_dir(dr):
        ar, ai = a_re[dr].astype(F32), a_im[dr].astype(F32)
        dt = jnp.exp(log_dt[dr].astype(F32))[:, None]
        mag = jnp.exp(ar * dt)
        abar_r, abar_i = mag * jnp.cos(ai * dt), mag * jnp.sin(ai * dt)
        den = ar * ar + ai * ai
        zr = ((abar_r - 1.0) * ar + abar_i * ai) / den
        zi = (abar_i * ar - (abar_r - 1.0) * ai) / den
        br, bi = b_re[dr].astype(F32), b_im[dr].astype(F32)
        bbar_r = zr[..., None] * br - zi[..., None] * bi
        bbar_i = zr[..., None] * bi + zi[..., None] * br

        def power(n):
            n = jnp.asarray(n, F32)[..., None, None]
            m = jnp.exp(n * (ar * dt))
            th = n * (ai * dt)
            return m * jnp.cos(th), m * jnp.sin(th)

        cr, ci = c_re[dr].astype(F32), c_im[dr].astype(F32)
        pr, pi_ = power(jnp.arange(C))
        cz_r = cr[None] * pr[:, :, None, :] - ci[None] * pi_[:, :, None, :]
        cz_i = cr[None] * pi_[:, :, None, :] + ci[None] * pr[:, :, None, :]
        kn = (jnp.einsum("ngip,gpj->gnij", cz_r, bbar_r, precision=HIGHEST)
              - jnp.einsum("ngip,gpj->gnij", cz_i, bbar_i, precision=HIGHEST))
        s_idx = jnp.arange(C)[:, None]
        t_idx = jnp.arange(C)[None, :]
        lag = (t_idx - s_idx) if dr == 0 else (s_idx - t_idx)
        valid = lag >= 0
        kt = kn[:, jnp.clip(lag, 0, C - 1)]
        kt = jnp.where(valid[None, :, :, None, None], kt, 0.0)
        tmat = kt.transpose(0, 1, 4, 2, 3).reshape(G, C * 16, C * 16)
        e = (C - 1 - jnp.arange(C)) if dr == 0 else jnp.arange(C)
        wr_, wi_ = power(e)
        w_re = wr_[..., None] * bbar_r[None] - wi_[..., None] * bbar_i[None]
        w_im = wr_[..., None] * bbar_i[None] + wi_[..., None] * bbar_r[None]
        wv = jnp.concatenate([w_re, w_im], axis=2)
        wv = wv.transpose(1, 0, 3, 2).reshape(G, C * 16, 2 * P)
        e = (jnp.arange(C) + 1) if dr == 0 else (C - jnp.arange(C))
        orr, oi = power(e)
        o_re = cr[None] * orr[:, :, None, :] - ci[None] * oi[:, :, None, :]
        o_im = -(cr[None] * oi[:, :, None, :] + ci[None] * orr[:, :, None, :])
        om = jnp.concatenate([o_re, o_im], axis=3)
        om = om.transpose(1, 3, 0, 2).reshape(G, 2 * P, C * 16)
        return tmat, wv, om

    t0, w0, o0 = per_dir(0)
    t1, w1, o1 = per_dir(1)
    tmat = t0 + t1
    wv = jnp.concatenate([w0, w1], axis=2)
    om = jnp.concatenate([o0, o1], axis=1)
    return tmat.astype(BF16), wv.astype(BF16), om.astype(BF16)


def _s5_scan_tables(a_re, a_im, log_dt, nlev):
    outs_r, outs_i = [], []
    for dr in range(2):
        ar, ai = a_re[dr].astype(F32), a_im[dr].astype(F32)
        dt = jnp.exp(log_dt[dr].astype(F32))[:, None]
        n = (S5_CHUNK * (2.0 ** jnp.arange(nlev, dtype=F32)))[:, None, None]
        m = jnp.exp(n * (ar * dt)[None])
        th = n * (ai * dt)[None]
        zr, zi = m * jnp.cos(th), m * jnp.sin(th)
        outs_r.append(jnp.concatenate([zr, zr], axis=-1))
        outs_i.append(jnp.concatenate([-zi, zi], axis=-1))
    zr = jnp.concatenate(outs_r, axis=-1).transpose(1, 0, 2)
    zi = jnp.concatenate(outs_i, axis=-1).transpose(1, 0, 2)
    return zr, zi


def _s5_kernel(x_ref, wv_ref, tm_ref, om_ref, zr_ref, zi_ref, y_ref, *, bsz, nc, nlev, p2):
    x = x_ref[...]
    v = jnp.dot(x, wv_ref[...], preferred_element_type=F32)
    row = lax.broadcasted_iota(jnp.int32, (nc, p2), 0)
    states = []
    for b in range(bsz):
        vb = v[b * nc:(b + 1) * nc]
        pf, pb = vb[:, :p2], vb[:, p2:]
        for k in range(nlev):
            d = 1 << k
            if d >= nc:
                break
            zr = zr_ref[k:k + 1, :]
            zi = zi_ref[k:k + 1, :]
            sf = jnp.where(row >= d, pltpu.roll(pf, d, 0), 0.0)
            pf = pf + zr[:, :p2] * sf + zi[:, :p2] * pltpu.roll(sf, p2 // 2, 1)
            sb = jnp.where(row < nc - d, pltpu.roll(pb, nc - d, 0), 0.0)
            pb = pb + zr[:, p2:] * sb + zi[:, p2:] * pltpu.roll(sb, p2 // 2, 1)
        sf = jnp.where(row >= 1, pltpu.roll(pf, 1, 0), 0.0)
        sb = jnp.where(row < nc - 1, pltpu.roll(pb, nc - 1, 0), 0.0)
        states.append(jnp.concatenate([sf, sb], axis=1))
    s = jnp.concatenate(states, axis=0).astype(BF16)
    y = jnp.dot(x, tm_ref[...], preferred_element_type=F32)
    y = y + jnp.dot(s, om_ref[...], preferred_element_type=F32)
    y_ref[...] = y


def s5_ssm(h, tables, scan_tables):
    bsz, L, D = h.shape
    C = S5_CHUNK
    G = D // S5_GROUP
    nc = L // C
    tmat, wv, om = tables
    zr, zi = scan_tables
    nlev = zr.shape[1]
    p2 = wv.shape[2] // 2
    cw = C * S5_GROUP
    xg = h.reshape(bsz, nc, C, G, S5_GROUP).transpose(3, 0, 1, 2, 4).reshape(G, bsz * nc, cw)
    y = pl.pallas_call(
        functools.partial(_s5_kernel, bsz=bsz, nc=nc, nlev=nlev, p2=p2),
        grid=(G,),
        in_specs=[pl.BlockSpec((None, bsz * nc, cw), lambda g: (g, 0, 0)),
                  pl.BlockSpec((None, cw, 2 * p2), lambda g: (g, 0, 0)),
                  pl.BlockSpec((None, cw, cw), lambda g: (g, 0, 0)),
                  pl.BlockSpec((None, 2 * p2, cw), lambda g: (g, 0, 0)),
                  pl.BlockSpec((None, nlev, 2 * p2), lambda g: (g, 0, 0)),
                  pl.BlockSpec((None, nlev, 2 * p2), lambda g: (g, 0, 0))],
        out_specs=pl.BlockSpec((None, bsz * nc, cw), lambda g: (g, 0, 0)),
        out_shape=jax.ShapeDtypeStruct((G, bsz * nc, cw), F32),
        compiler_params=_params(("parallel",)),
        name="s5_ssm",
    )(xg, wv, tmat, om, zr, zi)
    return y.reshape(G, bsz, nc, C, S5_GROUP).transpose(1, 2, 3, 0, 4).reshape(bsz, L, D)


def _gelu_skip_kernel(h_ref, y_ref, d_ref, o_ref):
    y = h_ref[...].astype(F32) * d_ref[...] + y_ref[...]
    o_ref[...] = jax.nn.gelu(y, approximate=True).astype(o_ref.dtype)


def gelu_skip(h, y, d_skip):
    bsz, L, D = h.shape
    tl = _tile(L, 512)
    spec = pl.BlockSpec((None, tl, D), lambda b, i: (b, i, 0))
    return pl.pallas_call(
        _gelu_skip_kernel, grid=(bsz, L // tl),
        in_specs=[spec, spec, pl.BlockSpec((1, D), lambda b, i: (0, 0))],
        out_specs=spec, out_shape=jax.ShapeDtypeStruct((bsz, L, D), BF16),
        compiler_params=_params(("parallel", "parallel")), name="gelu_skip",
    )(h, y, d_skip.reshape(1, D).astype(F32))


def _rel_bucket(rel):
    nb = REL_BUCKETS // 2
    max_exact = nb // 2
    n = jnp.abs(rel)
    large = max_exact + (jnp.log(jnp.maximum(n, 1).astype(F32) / max_exact)
                         / math.log(REL_MAX_DIST / max_exact) * (nb - max_exact)).astype(jnp.int32)
    large = jnp.minimum(large, nb - 1)
    return jnp.where(rel > 0, nb, 0) + jnp.where(n < max_exact, n, large)


def _bias_tiles_kernel(table_ref, bucket_ref, o_ref):
    h = pl.program_id(0)
    bk = bucket_ref[...]
    acc = jnp.zeros(bk.shape, F32)
    for b in range(REL_BUCKETS):
        acc = jnp.where(bk == b, table_ref[b, h], acc)
    o_ref[...] = acc * LOG2E


def bias_tiles(rel_bias, t):
    assert t >= REL_MAX_DIST
    H = rel_bias.shape[1]
    off = jnp.arange(-2, 3, dtype=jnp.int32)[:, None, None] * t
    rel = off + jnp.arange(t, dtype=jnp.int32)[None, None, :] - jnp.arange(t, dtype=jnp.int32)[None, :, None]
    buckets = _rel_bucket(rel)
    return pl.pallas_call(
        _bias_tiles_kernel, grid=(H, 5),
        in_specs=[pl.BlockSpec(memory_space=pltpu.SMEM),
                  pl.BlockSpec((None, t, t), lambda h, d: (d, 0, 0))],
        out_specs=pl.BlockSpec((None, None, t, t), lambda h, d: (h, d, 0, 0)),
        out_shape=jax.ShapeDtypeStruct((H, 5, t, t), F32),
        compiler_params=_params(("parallel", "parallel")), name="bias_tiles",
    )(rel_bias.astype(F32), buckets)


def _lane_tile(x, n):
    return jnp.concatenate([x] * n, axis=1)


def _attn_kernel(q_ref, k_ref, v_ref, bias_ref, lam_ref, g_ref, o_ref, m_ref, l_ref, acc_ref, *,
                 nkv, lambda_init):
    j = pl.program_id(3)

    @pl.when(j == 0)
    def _():
        m_ref[...] = jnp.full(m_ref.shape, -jnp.inf, F32)
        l_ref[...] = jnp.zeros(l_ref.shape, F32)
        acc_ref[...] = jnp.zeros(acc_ref.shape, F32)

    dh = DA_HEAD_DIM
    tq, tk = q_ref.shape[0], k_ref.shape[0]
    rep = v_ref.shape[1] // LANES

    def rows_block(r, carry):
        rows = pl.ds(pl.multiple_of(r * ATTN_ROWS, ATTN_ROWS), ATTN_ROWS)
        q = q_ref[rows, :]
        bias = bias_ref[rows, :]
        v = v_ref[...]
        for m in range(2):
            s = lax.dot_general(q[:, m * dh:(m + 1) * dh], k_ref[:, m * dh:(m + 1) * dh],
                                (((1,), (1,)), ((), ())), preferred_element_type=F32) + bias
            m_prev = m_ref[m, rows, :]
            m_new = jnp.maximum(m_prev, jnp.max(s, axis=-1, keepdims=True))
            alpha = jnp.exp2(m_prev - m_new)
            p = jnp.exp2(s - _lane_tile(m_new, tk // LANES))
            l_ref[m, rows, :] = alpha * l_ref[m, rows, :] + jnp.sum(p, axis=-1, keepdims=True)
            acc_ref[m, rows, :] = (_lane_tile(alpha, rep) * acc_ref[m, rows, :]
                                   + jnp.dot(p.astype(v.dtype), v, preferred_element_type=F32))
            m_ref[m, rows, :] = m_new
        return carry

    lax.fori_loop(0, tq // ATTN_ROWS, rows_block, 0, unroll=ATTN_UNROLL)

    @pl.when(j == nkv - 1)
    def _():
        lam_p = lam_ref[...]
        lam = (jnp.exp(jnp.sum(lam_p[0:1] * lam_p[1:2], axis=-1, keepdims=True))
               - jnp.exp(jnp.sum(lam_p[2:3] * lam_p[3:4], axis=-1, keepdims=True)) + lambda_init)
        o = (acc_ref[0] / _lane_tile(l_ref[0], rep) - lam * (acc_ref[1] / _lane_tile(l_ref[1], rep)))
        o = o * lax.rsqrt(jnp.mean(o * o, axis=-1, keepdims=True) + EPS) * g_ref[...]
        o_ref[...] = (o * (1.0 - lambda_init)).astype(o_ref.dtype)


def diff_attention_core(qkv, bias, lam_params, subln_g, lambda_init, t):
    bsz, L, D3 = qkv.shape
    D = D3 // 3
    hw = 2 * DA_HEAD_DIM
    H = D // hw
    nq = L // t
    return pl.pallas_call(
        functools.partial(_attn_kernel, nkv=nq, lambda_init=lambda_init),
        grid=(bsz, H, nq, nq),
        in_specs=[pl.BlockSpec((None, t, hw), lambda b, h, i, j: (b, i, h)),
                  pl.BlockSpec((None, t, hw), lambda b, h, i, j: (b, j, H + h)),
                  pl.BlockSpec((None, t, hw), lambda b, h, i, j: (b, j, 2 * H + h)),
                  pl.BlockSpec((None, None, t, t),
                               lambda b, h, i, j: (h, jnp.clip(j - i, -2, 2) + 2, 0, 0)),
                  pl.BlockSpec((4, DA_HEAD_DIM), lambda b, h, i, j: (0, 0)),
                  pl.BlockSpec((1, hw), lambda b, h, i, j: (0, 0))],
        out_specs=pl.BlockSpec((None, t, hw), lambda b, h, i, j: (b, i, h)),
        out_shape=jax.ShapeDtypeStruct((bsz, L, D), BF16),
        scratch_shapes=[pltpu.VMEM((2, t, LANES), F32), pltpu.VMEM((2, t, LANES), F32),
                        pltpu.VMEM((2, t, hw), F32)],
        compiler_params=_params(("parallel", "parallel", "parallel", "arbitrary")),
        name="diff_attention",
    )(qkv, qkv, qkv, bias, lam_params, subln_g.reshape(1, hw).astype(F32))


def _hgrn_kernel(q_ref, f_ref, i_ref, gate_ref, lb_ref, ng_ref, tri_ref, o_ref, ofw_ref, s_ref, *,
                 tl, n_tiles, hb):
    dr = pl.program_id(2)
    t = pl.program_id(3)
    C = HG_CHUNK
    nch = tl // C
    W = hb * HG_DIM
    fwd = dr == 0

    @pl.when(t == 0)
    def _():
        s_ref[...] = jnp.zeros(s_ref.shape, F32)

    lb = lb_ref[...]
    sig = jax.nn.sigmoid(f_ref[...])
    logf = jnp.log(lb + (1.0 - lb) * sig)
    kk = (1.0 - lb) * (1.0 - sig)
    hi = logf.astype(BF16)
    lo = (logf - hi.astype(F32)).astype(BF16)
    tri = tri_ref[...]
    g = (jnp.dot(tri, hi, preferred_element_type=F32) + jnp.dot(tri, lo, preferred_element_type=F32))
    g3 = g.reshape(nch, C, W)
    g_tot = jnp.sum(logf.reshape(nch, C, W), axis=1, keepdims=True)
    qt = (q_ref[...] * jnp.exp(g)).astype(BF16)
    kt = (kk * jnp.exp(-g)).astype(BF16)
    kd = (kk.reshape(nch, C, W) * jnp.exp(g_tot - g3)).reshape(tl, W).astype(BF16)
    dec = jnp.exp(g_tot).reshape(nch, W)
    inp = i_ref[...].astype(BF16)
    row = lax.broadcasted_iota(jnp.int32, (C, C), 0)
    col = lax.broadcasted_iota(jnp.int32, (C, C), 1)
    causal = jnp.where(fwd, row - col, col - row) >= 0
    tile_idx = jnp.where(fwd, t, n_tiles - 1 - t)
    base = pl.multiple_of(tile_idx * tl, C)
    nt_dims = (((1,), (1,)), ((), ()))
    tn_dims = (((0,), (0,)), ((), ()))

    def run(order, is_fwd):
        for c in order:
            r = slice(c * C, (c + 1) * C)
            for hh in range(hb):
                cs = slice(hh * HG_DIM, (hh + 1) * HG_DIM)
                qc, kc, ic, kdc = qt[r, cs], kt[r, cs], inp[r, cs], kd[r, cs]
                att = lax.dot_general(qc, kc, nt_dims, preferred_element_type=F32)
                att = jnp.where(causal, att, 0.0).astype(BF16)
                st = s_ref[hh]
                o = (jnp.dot(att, ic, preferred_element_type=F32)
                     + lax.dot_general(qc, st.astype(BF16), nt_dims, preferred_element_type=F32))
                s_ref[hh] = (dec[c:c + 1, cs] * st
                             + lax.dot_general(ic, kdc, tn_dims, preferred_element_type=F32))
                rows = pl.ds(pl.multiple_of(base + c * C, C), C)
                if is_fwd:
                    ofw_ref[rows, cs] = o
                else:
                    tot = ofw_ref[rows, cs] + o
                    y = tot * lax.rsqrt(jnp.mean(tot * tot, axis=-1, keepdims=True) + EPS) * ng_ref[...]
                    gt = gate_ref[r, cs]
                    o_ref[r, cs] = (y * (gt * jax.nn.sigmoid(gt))).astype(o_ref.dtype)

    @pl.when(fwd)
    def _():
        run(range(nch), True)

    @pl.when(jnp.logical_not(fwd))
    def _():
        run(range(nch - 1, -1, -1), False)


def hgrn2_core(proj, lb, norm_g, *, tl=512, hb=HG_HEADS_PER_STEP):
    bsz, L, D5 = proj.shape
    D = D5 // 5
    H = D // HG_DIM
    hb = min(hb, H)
    HB = H // hb
    W = hb * HG_DIM
    tl = _tile(L, tl)
    nt = L // tl
    C = HG_CHUNK
    idx = jnp.arange(tl, dtype=jnp.int32)
    same = (idx[:, None] // C) == (idx[None, :] // C)
    tri = jnp.stack([same & (idx[None, :] <= idx[:, None]),
                     same & (idx[None, :] >= idx[:, None])]).astype(BF16)

    def tile_of(dr, t):
        return jnp.where(dr == 0, t, nt - 1 - t)

    def in_spec(section):
        return pl.BlockSpec((None, tl, W), lambda b, h, dr, t: (b, tile_of(dr, t), section * HB + h))

    f_spec = pl.BlockSpec((None, tl, W), lambda b, h, dr, t: (b, tile_of(dr, t), (1 + dr) * HB + h))
    out_spec = pl.BlockSpec((None, tl, W),
                            lambda b, h, dr, t: (b, jnp.where(dr == 0, nt - 1, nt - 1 - t), h))
    return pl.pallas_call(
        functools.partial(_hgrn_kernel, tl=tl, n_tiles=nt, hb=hb),
        grid=(bsz, HB, 2, nt),
        in_specs=[in_spec(0), f_spec, in_spec(3), in_spec(4),
                  pl.BlockSpec((1, W), lambda b, h, dr, t: (0, h)),
                  pl.BlockSpec((1, HG_DIM), lambda b, h, dr, t: (0, 0)),
                  pl.BlockSpec((None, tl, tl), lambda b, h, dr, t: (dr, 0, 0))],
        out_specs=out_spec,
        out_shape=jax.ShapeDtypeStruct((bsz, L, D), BF16),
        scratch_shapes=[pltpu.VMEM((L, W), F32), pltpu.VMEM((hb, HG_DIM, HG_DIM), F32)],
        compiler_params=_params(("parallel", "parallel", "arbitrary", "arbitrary")),
        name="hgrn2",
    )(proj, proj, proj, proj, lb.reshape(1, D).astype(F32), norm_g.reshape(1, HG_DIM).astype(F32), tri)


def _pool_kernel(h_ref, o_ref, pad_ref, *, L):
    grp = pl.program_id(1)
    x = h_ref[...].astype(F32)
    zeros = jnp.zeros((POOL_HALO, x.shape[1]), F32)
    pad_ref[0:POOL_HALO, :] = zeros
    pad_ref[POOL_HALO:POOL_HALO + L, :] = x
    pad_ref[POOL_HALO + L:POOL_HALO + L + POOL_HALO, :] = zeros
    pos = lax.broadcasted_iota(jnp.int32, (L, 1), 0)
    for gi, w in enumerate(POOL_WINDOWS):
        @pl.when(grp == gi)
        def _(w=w):
            acc = jnp.zeros(x.shape, F32)
            for d in range(-(w // 2), w - w // 2):
                acc = acc + pad_ref[POOL_HALO + d:POOL_HALO + d + L, :]
            lo = jnp.clip(pos - w // 2, 0, L)
            hi = jnp.clip(pos + w - w // 2, 0, L)
            cnt = (hi - lo).astype(F32)
            o_ref[...] = (acc / cnt - x).astype(o_ref.dtype)


def pool_core(h, *, width=128):
    bsz, L, D = h.shape
    ng = len(POOL_WINDOWS)
    dg = D // ng
    width = _tile(dg, width)
    per_g = dg // width
    spec = pl.BlockSpec((None, L, width), lambda b, g, s: (b, 0, g * per_g + s))
    return pl.pallas_call(
        functools.partial(_pool_kernel, L=L),
        grid=(bsz, ng, per_g),
        in_specs=[spec], out_specs=spec,
        out_shape=jax.ShapeDtypeStruct((bsz, L, D), BF16),
        scratch_shapes=[pltpu.VMEM((L + 2 * POOL_HALO, width), F32)],
        compiler_params=_params(("parallel", "parallel", "parallel")),
        name="pool",
    )(h)


def _trunk(x, c, p, shared):
    bsz, L, D = x.shape
    T = bsz * L
    depth = p["norm1_g"].shape[0]
    c_low = small_matmul(c, p["w_ada_down"], silu_in=True)
    for l in range(depth):
        mod = small_matmul(c_low, p["w_ada"][l], p["b_ada"][l]).reshape(bsz, 6, 1, D)
        h = norm_mod(x, p["norm1_g"][l], mod, 1, 0)
        kind, j = l % 4, l // 4
        x2 = x.reshape(T, D)
        if kind == 0:
            y = s5_ssm(h, shared["s5_tables"][j], shared["s5_scan"][(j, L)])
            z = gelu_skip(h, y, p["s5_d"][j])
            x = glu_linear(z.reshape(T, D), shared["s5_w_glu"][j], "sigglu", out_dtype=F32,
                           res=x2, gate=mod, gate_idx=2, rows_per_batch=L, name="s5_glu")
        elif kind == 1:
            lambda_init = 0.8 - 0.6 * math.exp(-0.3 * l)
            qkv = linear(h.reshape(T, D), shared["da_w_qkv"][j], name="da_qkv")
            lam_params = jnp.stack([p["da_lam_q1"][j], p["da_lam_k1"][j],
                                    p["da_lam_q2"][j], p["da_lam_k2"][j]]).astype(F32)
            o = diff_attention_core(qkv.reshape(bsz, L, 3 * D), shared["bias_tiles"], lam_params,
                                    p["da_subln_g"][j], lambda_init, shared["attn_tile"])
            x = linear(o.reshape(T, D), shared["da_w_o"][j], out_dtype=F32, res=x2, gate=mod,
                       gate_idx=2, rows_per_batch=L, name="da_out")
        elif kind == 2:
            proj = linear(h.reshape(T, D), shared["hg_w_in"][j], out_dtype=F32, name="hg_in")
            o = hgrn2_core(proj.reshape(bsz, L, 5 * D), shared["hg_lb"][l], p["hg_norm_g"][j])
            x = linear(o.reshape(T, D), shared["hg_w_o"][j], out_dtype=F32, res=x2, gate=mod,
                       gate_idx=2, rows_per_batch=L, name="hg_out")
        else:
            pooled = pool_core(h)
            x = pool_linear(pooled.reshape(T, D), shared["pool_w"][j], p["pool_scale"][j].astype(F32),
                            x2, mod, 2, L)
        x = x.reshape(bsz, L, D)
        x2 = x.reshape(T, D)
        if l % 2 == 0:
            h = norm_mod(x, p["norm2_g"][l], mod, 4, 3)
            hid = glu_linear(h.reshape(T, D), shared["ff_w_in"][l // 2], "swiglu", name="ff_in")
            x = linear(hid, shared["ff_w_out"][l // 2], tk=_ff_tk(hid.shape[1]), out_dtype=F32,
                       res=x2, gate=mod, gate_idx=5, rows_per_batch=L, name="ff_out")
        else:
            h, route, counts = norm_mod(x, p["norm2_g"][l], mod, 4, 3, w_router=p["moe_router"][l // 2])
            x = moe_ffn(h.reshape(T, D), route.reshape(T, LANES), counts, shared["moe_w_in"][l // 2],
                        shared["moe_w_out"][l // 2], x2, mod, 5, L)
        x = x.reshape(bsz, L, D)
    return final_norm(x, p["final_g"])


def _ff_tk(k):
    return k // 2 if (k // 2) % LANES == 0 else k


def kernel(x_prompt, x_sample, c_prompt, c_sample, norm1_g, norm2_g, final_g, w_ada_down, w_ada, b_ada,
           s5_a_re, s5_a_im, s5_log_dt, s5_b_re, s5_b_im, s5_c_re, s5_c_im, s5_d, s5_w_glu,
           da_w_qkv, da_w_o, da_lam_q1, da_lam_k1, da_lam_q2, da_lam_k2, da_subln_g, rel_bias,
           hg_w_in, hg_w_o, hg_norm_g, hg_lb_logits,
           pool_w, pool_scale,
           ff_w_in, ff_w_out, moe_router, moe_w_in, moe_w_out):
    p = dict(norm1_g=norm1_g, norm2_g=norm2_g, final_g=final_g, w_ada_down=w_ada_down, w_ada=w_ada,
             b_ada=b_ada, s5_d=s5_d, da_lam_q1=da_lam_q1, da_lam_k1=da_lam_k1, da_lam_q2=da_lam_q2,
             da_lam_k2=da_lam_k2, da_subln_g=da_subln_g, hg_norm_g=hg_norm_g, pool_scale=pool_scale,
             moe_router=moe_router)
    seq_lens = sorted({x_prompt.shape[1], x_sample.shape[1]})
    attn_tile = _tile(min(seq_lens), 512)
    lb_cum = jnp.cumsum(jax.nn.softmax(hg_lb_logits.astype(F32), axis=0), axis=0)
    hg_lb = jnp.concatenate([jnp.zeros_like(lb_cum[:1]), lb_cum[:-1]], axis=0)
    s5_scan = {}
    for j in range(s5_a_re.shape[0]):
        for L in seq_lens:
            nlev = max(1, math.ceil(math.log2(L // S5_CHUNK)))
            s5_scan[(j, L)] = _s5_scan_tables(s5_a_re[j], s5_a_im[j], s5_log_dt[j], nlev)
    d_model = x_prompt.shape[-1]
    qkv_scale = jnp.concatenate([jnp.full((d_model,), DA_HEAD_DIM ** -0.5 * LOG2E, F32),
                                 jnp.ones((2 * d_model,), F32)])
    shared = dict(
        s5_tables=[_s5_tables(s5_a_re[j], s5_a_im[j], s5_log_dt[j], s5_b_re[j], s5_b_im[j],
                              s5_c_re[j], s5_c_im[j]) for j in range(s5_a_re.shape[0])],
        s5_scan=s5_scan,
        s5_w_glu=s5_w_glu.astype(BF16), da_w_qkv=(da_w_qkv * qkv_scale).astype(BF16),
        da_w_o=da_w_o.astype(BF16),
        hg_w_in=hg_w_in.astype(BF16), hg_w_o=hg_w_o.astype(BF16), pool_w=pool_w.astype(BF16),
        ff_w_in=ff_w_in.astype(BF16), ff_w_out=ff_w_out.astype(BF16),
        moe_w_in=moe_w_in.astype(BF16), moe_w_out=moe_w_out.astype(BF16),
        hg_lb=hg_lb, attn_tile=attn_tile, bias_tiles=bias_tiles(rel_bias, attn_tile),
    )
    y_prompt = _trunk(x_prompt, c_prompt, p, shared)
    y_sample = _trunk(x_sample, c_sample, p, shared)
    return (y_prompt, y_sample)
```

```python
import functools
import math

import jax
import jax.numpy as jnp
from jax import lax
from jax.experimental import pallas as pl
from jax.experimental.pallas import tpu as pltpu

F32 = jnp.float32
BF16 = jnp.bfloat16
EPS = 1e-6
LANES = 128
V7X_VMEM_LIMIT = 60 * 1024 * 1024

S5_GROUP = 16
S5_CHUNK = 16
DA_HEAD_DIM = 128
ATTN_ROWS = 128
ATTN_UNROLL = 4
REL_BUCKETS = 32
REL_MAX_DIST = 128
HG_DIM = 128
HG_CHUNK = 64
HG_HEADS_PER_STEP = 4
POOL_WINDOWS = (2, 4, 8, 16)
POOL_HALO = 8
MOE_ROW_TILE = 512
HIGHEST = lax.Precision.HIGHEST
LOG2E = math.log2(math.e)


def _params(sem, vmem=V7X_VMEM_LIMIT):
    return pltpu.CompilerParams(dimension_semantics=sem, vmem_limit_bytes=vmem)


def _tile(n, pref):
    t = min(n, pref)
    while n % t:
        t //= 2
    return t


def _small_mm_kernel(a_ref, w_ref, b_ref, o_ref, *, silu_in):
    a = a_ref[...]
    if silu_in:
        a = a * jax.nn.sigmoid(a)
    o_ref[...] = jnp.dot(a, w_ref[...], preferred_element_type=F32, precision=HIGHEST) + b_ref[...]


def small_matmul(a, w, b=None, *, silu_in=False):
    m, k = a.shape
    n = w.shape[1]
    tn = _tile(n, 2048)
    if b is None:
        b = jnp.zeros((1, n), F32)
    return pl.pallas_call(
        functools.partial(_small_mm_kernel, silu_in=silu_in),
        grid=(n // tn,),
        in_specs=[pl.BlockSpec((m, k), lambda j: (0, 0)),
                  pl.BlockSpec((k, tn), lambda j: (0, j)),
                  pl.BlockSpec((1, tn), lambda j: (0, j))],
        out_specs=pl.BlockSpec((m, tn), lambda j: (0, j)),
        out_shape=jax.ShapeDtypeStruct((m, n), F32),
        compiler_params=_params(("parallel",)),
        name="small_matmul",
    )(a, w, b.reshape(1, n))


def _norm_mod_kernel(x_ref, g_ref, sc_ref, sh_ref, o_ref):
    x = x_ref[...]
    y = x * lax.rsqrt(jnp.mean(x * x, axis=-1, keepdims=True) + EPS) * g_ref[...]
    o_ref[...] = (y * (1.0 + sc_ref[...]) + sh_ref[...]).astype(o_ref.dtype)


ROUTE_E1, ROUTE_E2, ROUTE_R1, ROUTE_R2, ROUTE_G1, ROUTE_G2 = range(6)


def _norm_mod_router_kernel(x_ref, g_ref, sc_ref, sh_ref, wr_ref, tri_ref, o_ref, route_ref, cnt_ref,
                            run_ref, *, n_experts):
    @pl.when((pl.program_id(0) == 0) & (pl.program_id(1) == 0))
    def _():
        run_ref[...] = jnp.zeros(run_ref.shape, F32)

    x = x_ref[...]
    y = x * lax.rsqrt(jnp.mean(x * x, axis=-1, keepdims=True) + EPS) * g_ref[...]
    h = y * (1.0 + sc_ref[...]) + sh_ref[...]
    o_ref[...] = h.astype(o_ref.dtype)
    logits = jnp.dot(h, wr_ref[...], preferred_element_type=F32, precision=HIGHEST)
    lane = lax.broadcasted_iota(jnp.int32, logits.shape, 1)
    neg = jnp.float32(-jnp.inf)
    logits = jnp.where(lane < n_experts, logits, neg)
    v1 = jnp.max(logits, axis=-1, keepdims=True)
    i1 = jnp.min(jnp.where(logits == v1, lane, LANES), axis=-1, keepdims=True)
    rest = jnp.where(lane == i1, neg, logits)
    v2 = jnp.max(rest, axis=-1, keepdims=True)
    i2 = jnp.min(jnp.where(rest == v2, lane, LANES), axis=-1, keepdims=True)
    e2 = jnp.exp(v2 - v1)
    g1 = 1.0 / (1.0 + e2)
    g2 = e2 / (1.0 + e2)
    sel = jnp.where((lane == i1) | (lane == i2), 1.0, 0.0)
    before = run_ref[...] + jnp.dot(tri_ref[...], sel.astype(BF16), preferred_element_type=F32)
    r1 = jnp.sum(jnp.where(lane == i1, before, 0.0), axis=-1, keepdims=True)
    r2 = jnp.sum(jnp.where(lane == i2, before, 0.0), axis=-1, keepdims=True)
    run_ref[...] = run_ref[...] + jnp.sum(sel, axis=0, keepdims=True)
    cnt_ref[...] = run_ref[...]
    rec = jnp.zeros(logits.shape, F32)
    for k, val in ((ROUTE_E1, i1.astype(F32)), (ROUTE_E2, i2.astype(F32)), (ROUTE_R1, r1),
                   (ROUTE_R2, r2), (ROUTE_G1, g1), (ROUTE_G2, g2)):
        rec = jnp.where(lane == k, val, rec)
    route_ref[...] = rec


def norm_mod(x, g, mod, sc_idx, sh_idx, w_router=None):
    bsz, L, D = x.shape
    tl = _tile(L, 256)
    grid = (bsz, L // tl)
    x_spec = pl.BlockSpec((None, tl, D), lambda b, i: (b, i, 0))
    g_spec = pl.BlockSpec((1, D), lambda b, i: (0, 0))
    sc_spec = pl.BlockSpec((None, None, 1, D), lambda b, i: (b, sc_idx, 0, 0))
    sh_spec = pl.BlockSpec((None, None, 1, D), lambda b, i: (b, sh_idx, 0, 0))
    h_shape = jax.ShapeDtypeStruct((bsz, L, D), BF16)
    if w_router is None:
        return pl.pallas_call(
            _norm_mod_kernel, grid=grid,
            in_specs=[x_spec, g_spec, sc_spec, sh_spec],
            out_specs=x_spec, out_shape=h_shape,
            compiler_params=_params(("parallel", "parallel")), name="norm_mod",
        )(x, g.reshape(1, D), mod, mod)
    n_experts = w_router.shape[1]
    wr = jnp.zeros((D, LANES), F32).at[:, :n_experts].set(w_router)
    idx = jnp.arange(tl, dtype=jnp.int32)
    tri = (idx[None, :] < idx[:, None]).astype(BF16)
    return pl.pallas_call(
        functools.partial(_norm_mod_router_kernel, n_experts=n_experts), grid=grid,
        in_specs=[x_spec, g_spec, sc_spec, sh_spec, pl.BlockSpec((D, LANES), lambda b, i: (0, 0)),
                  pl.BlockSpec((tl, tl), lambda b, i: (0, 0))],
        out_specs=[x_spec, pl.BlockSpec((None, tl, LANES), lambda b, i: (b, i, 0)),
                   pl.BlockSpec((1, LANES), lambda b, i: (0, 0))],
        out_shape=[jax.ShapeDtypeStruct((bsz, L, D), F32), jax.ShapeDtypeStruct((bsz, L, LANES), F32),
                   jax.ShapeDtypeStruct((1, LANES), F32)],
        scratch_shapes=[pltpu.VMEM((1, LANES), F32)],
        compiler_params=_params(("arbitrary", "arbitrary")), name="norm_mod_router",
    )(x, g.reshape(1, D), mod, mod, wr, tri)


def _final_norm_kernel(x_ref, g_ref, o_ref):
    x = x_ref[...]
    o_ref[...] = x * lax.rsqrt(jnp.mean(x * x, axis=-1, keepdims=True) + EPS) * g_ref[...]


def final_norm(x, g):
    bsz, L, D = x.shape
    tl = _tile(L, 256)
    spec = pl.BlockSpec((None, tl, D), lambda b, i: (b, i, 0))
    return pl.pallas_call(
        _final_norm_kernel, grid=(bsz, L // tl),
        in_specs=[spec, pl.BlockSpec((1, D), lambda b, i: (0, 0))],
        out_specs=spec, out_shape=jax.ShapeDtypeStruct((bsz, L, D), F32),
        compiler_params=_params(("parallel", "parallel")), name="final_norm",
    )(x, g.reshape(1, D))


def _mm_kernel(*refs, act, nk, has_res, has_colscale):
    it = iter(refs)
    a_ref = next(it)
    w1_ref = next(it)
    w2_ref = next(it) if act else None
    res_ref = next(it) if has_res else None
    gate_ref = next(it) if has_res else None
    cs_ref = next(it) if has_colscale else None
    o_ref = next(it)
    acc1_ref = next(it) if nk > 1 else None
    acc2_ref = next(it) if (nk > 1 and act) else None
    k = pl.program_id(2)

    a = a_ref[...]
    p1 = jnp.dot(a, w1_ref[...], preferred_element_type=F32)
    p2 = jnp.dot(a, w2_ref[...], preferred_element_type=F32) if act else None

    def finish(y1, y2):
        if act == "swiglu":
            y = y1 * jax.nn.sigmoid(y1) * y2
        elif act == "sigglu":
            y = y1 * jax.nn.sigmoid(y2)
        else:
            y = y1
        if has_colscale:
            y = y * cs_ref[...]
        if has_res:
            y = res_ref[...] + gate_ref[...] * y
        o_ref[...] = y.astype(o_ref.dtype)

    if nk == 1:
        finish(p1, p2)
    else:
        @pl.when(k == 0)
        def _():
            acc1_ref[...] = p1
            if act:
                acc2_ref[...] = p2

        @pl.when(k > 0)
        def _():
            acc1_ref[...] += p1
            if act:
                acc2_ref[...] += p2

        @pl.when(k == nk - 1)
        def _():
            finish(acc1_ref[...], acc2_ref[...] if act else None)


def _mm_call(a, weights, w_specs, n_out, *, tm, tn, nk, a_spec, act, out_dtype,
             res, gate, gate_idx, rows_per_batch, colscale, name):
    M = a.shape[0]
    grid = (M // tm, n_out // tn, nk)
    in_specs = [a_spec] + list(w_specs)
    args = [a] + list(weights)
    if res is not None:
        tiles_per_batch = rows_per_batch // tm
        in_specs.append(pl.BlockSpec((tm, tn), lambda i, j, k: (i, j)))
        in_specs.append(pl.BlockSpec((None, None, 1, tn),
                                     lambda i, j, k: (i // tiles_per_batch, gate_idx, 0, j)))
        args += [res, gate]
    if colscale is not None:
        in_specs.append(pl.BlockSpec((1, tn), lambda i, j, k: (0, j)))
        args.append(colscale.reshape(1, n_out))
    scratch = []
    if nk > 1:
        scratch.append(pltpu.VMEM((tm, tn), F32))
        if act:
            scratch.append(pltpu.VMEM((tm, tn), F32))
    return pl.pallas_call(
        functools.partial(_mm_kernel, act=act, nk=nk, has_res=res is not None,
                          has_colscale=colscale is not None),
        grid=grid, in_specs=in_specs,
        out_specs=pl.BlockSpec((tm, tn), lambda i, j, k: (i, j)),
        out_shape=jax.ShapeDtypeStruct((M, n_out), out_dtype),
        scratch_shapes=scratch,
        compiler_params=_params(("parallel", "parallel", "arbitrary")),
        name=name,
    )(*args)


def linear(a, w, *, tm=1024, tn=1024, tk=None, out_dtype=BF16, res=None, gate=None, gate_idx=0,
           rows_per_batch=None, name="linear"):
    M, K = a.shape
    N = w.shape[1]
    tm, tn = _tile(M if rows_per_batch is None else rows_per_batch, tm), _tile(N, tn)
    tk = K if tk is None else tk
    nk = K // tk
    return _mm_call(a, [w], [pl.BlockSpec((tk, tn), lambda i, j, k: (k, j))], N,
                    tm=tm, tn=tn, nk=nk, a_spec=pl.BlockSpec((tm, tk), lambda i, j, k: (i, k)),
                    act=None, out_dtype=out_dtype, res=res, gate=gate, gate_idx=gate_idx,
                    rows_per_batch=rows_per_batch, colscale=None, name=name)


def glu_linear(a, w, act, *, tm=1024, tn=512, out_dtype=BF16, res=None, gate=None, gate_idx=0,
               rows_per_batch=None, name="glu_linear"):
    M, K = a.shape
    F = w.shape[1] // 2
    tm, tn = _tile(M if rows_per_batch is None else rows_per_batch, tm), _tile(F, tn)
    off = F // tn
    specs = [pl.BlockSpec((K, tn), lambda i, j, k: (0, j)),
             pl.BlockSpec((K, tn), lambda i, j, k: (0, j + off))]
    return _mm_call(a, [w, w], specs, F, tm=tm, tn=tn, nk=1,
                    a_spec=pl.BlockSpec((tm, K), lambda i, j, k: (i, 0)),
                    act=act, out_dtype=out_dtype, res=res, gate=gate, gate_idx=gate_idx,
                    rows_per_batch=rows_per_batch, colscale=None, name=name)


def moe_plan(route, counts, n_experts, tm):
    T = route.shape[0]
    e1, e2 = route[:, ROUTE_E1].astype(jnp.int32), route[:, ROUTE_E2].astype(jnp.int32)
    r1, r2 = route[:, ROUTE_R1].astype(jnp.int32), route[:, ROUTE_R2].astype(jnp.int32)
    cnt = counts[0, :n_experts].astype(jnp.int32)
    gsz = (cnt + tm - 1) // tm * tm
    ends = jnp.cumsum(gsz)
    offs = ends - gsz

    def start_of(e):
        return sum(jnp.where(e == k, offs[k], 0) for k in range(n_experts))

    n_tiles = (2 * T) // tm + n_experts
    starts = jnp.arange(n_tiles, dtype=jnp.int32) * tm
    tile_e = jnp.minimum(jnp.sum((starts[:, None] >= ends[None, :]).astype(jnp.int32), axis=1),
                         n_experts - 1)
    n_valid = (ends[-1] // tm).reshape(1)
    d1, d2 = start_of(e1) + r1, start_of(e2) + r2
    tok = jnp.arange(T, dtype=jnp.int32)
    src = jnp.zeros((n_tiles * tm,), jnp.int32).at[d1].set(tok).at[d2].set(tok)
    return d1, d2, src, tile_e, n_valid


def _row_copies(src_at, dst_at, sem, n, start):
    def body(r, carry):
        for cp in (pltpu.make_async_copy(s_, d_, sem) for s_, d_ in zip(src_at(r), dst_at(r))):
            cp.start() if start else cp.wait()
        return carry
    lax.fori_loop(0, n, body, 0)


def _moe_up_kernel(src_ref, te_ref, nv_ref, h_ref, wg_ref, wu_ref, o_ref, a_ref, accg_ref, accu_ref, sem,
                   *, nk, tm, tk):
    del te_ref
    i = pl.program_id(0)
    k = pl.program_id(1)
    valid = i < nv_ref[0]

    @pl.when(valid & (k == 0))
    def _():
        def src_at(r):
            return (h_ref.at[pl.ds(src_ref[i * tm + r], 1), :],)

        def dst_at(r):
            return (a_ref.at[pl.ds(r, 1), :],)

        _row_copies(src_at, dst_at, sem, tm, True)
        _row_copies(src_at, dst_at, sem, tm, False)

    @pl.when(valid)
    def _():
        a = a_ref[:, pl.ds(pl.multiple_of(k * tk, tk), tk)].astype(BF16)
        pg = jnp.dot(a, wg_ref[...], preferred_element_type=F32)
        pu = jnp.dot(a, wu_ref[...], preferred_element_type=F32)

        @pl.when(k == 0)
        def _():
            accg_ref[...] = pg
            accu_ref[...] = pu

        @pl.when(k > 0)
        def _():
            accg_ref[...] += pg
            accu_ref[...] += pu

        @pl.when(k == nk - 1)
        def _():
            y = accg_ref[...]
            o_ref[...] = (y * jax.nn.sigmoid(y) * accu_ref[...]).astype(o_ref.dtype)

    @pl.when(jnp.logical_not(valid) & (k == nk - 1))
    def _():
        o_ref[...] = jnp.zeros(o_ref.shape, o_ref.dtype)


def moe_up(h, w_in, src, tile_e, n_valid, *, tm, tk=2048):
    K = h.shape[1]
    R = src.shape[0]
    F = w_in.shape[2] // 2
    tk = _tile(K, tk)
    nk = K // tk
    return pl.pallas_call(
        functools.partial(_moe_up_kernel, nk=nk, tm=tm, tk=tk),
        grid_spec=pltpu.PrefetchScalarGridSpec(
            num_scalar_prefetch=3, grid=(R // tm, nk),
            in_specs=[pl.BlockSpec(memory_space=pl.ANY),
                      pl.BlockSpec((None, tk, F), lambda i, k, src, te, nv: (te[i], k, 0)),
                      pl.BlockSpec((None, tk, F), lambda i, k, src, te, nv: (te[i], k, 1))],
            out_specs=pl.BlockSpec((tm, F), lambda i, k, src, te, nv: (i, 0)),
            scratch_shapes=[pltpu.VMEM((tm, K), F32), pltpu.VMEM((tm, F), F32), pltpu.VMEM((tm, F), F32),
                            pltpu.SemaphoreType.DMA(())]),
        out_shape=jax.ShapeDtypeStruct((R, F), BF16),
        compiler_params=_params(("arbitrary", "arbitrary")), name="moe_up",
    )(src, tile_e, n_valid, h, w_in, w_in)


def _moe_down_kernel(te_ref, nv_ref, a_ref, w_ref, o_ref):
    del te_ref
    valid = pl.program_id(0) < nv_ref[0]

    @pl.when(valid)
    def _():
        o_ref[...] = jnp.dot(a_ref[...], w_ref[...], preferred_element_type=F32)

    @pl.when(jnp.logical_not(valid))
    def _():
        o_ref[...] = jnp.zeros(o_ref.shape, o_ref.dtype)


def moe_down(hs, w_out, tile_e, n_valid, *, tm, tn=1024):
    R, F = hs.shape
    D = w_out.shape[2]
    tn = _tile(D, tn)
    return pl.pallas_call(
        _moe_down_kernel,
        grid_spec=pltpu.PrefetchScalarGridSpec(
            num_scalar_prefetch=2, grid=(R // tm, D // tn),
            in_specs=[pl.BlockSpec((tm, F), lambda i, j, te, nv: (i, 0)),
                      pl.BlockSpec((None, F, tn), lambda i, j, te, nv: (te[i], 0, j))],
            out_specs=pl.BlockSpec((tm, tn), lambda i, j, te, nv: (i, j))),
        out_shape=jax.ShapeDtypeStruct((R, D), F32),
        compiler_params=_params(("parallel", "parallel")), name="moe_down",
    )(tile_e, n_valid, hs, w_out)


def _moe_combine_kernel(d1_ref, d2_ref, ys_ref, route_ref, res_ref, gate_ref, o_ref, buf_ref, sem, *, tl):
    base = pl.program_id(0) * tl

    def src_at(r):
        return ys_ref.at[pl.ds(d1_ref[base + r], 1), :], ys_ref.at[pl.ds(d2_ref[base + r], 1), :]

    def dst_at(r):
        return buf_ref.at[0, pl.ds(r, 1), :], buf_ref.at[1, pl.ds(r, 1), :]

    _row_copies(src_at, dst_at, sem, tl, True)
    _row_copies(src_at, dst_at, sem, tl, False)
    route = route_ref[...]
    g1 = route[:, ROUTE_G1:ROUTE_G1 + 1]
    g2 = route[:, ROUTE_G2:ROUTE_G2 + 1]
    o_ref[...] = res_ref[...] + gate_ref[...] * (g1 * buf_ref[0] + g2 * buf_ref[1])


def moe_combine(ys, route, d1, d2, res, gate, gate_idx, rows_per_batch, *, tl=256):
    T, D = res.shape
    tl = _tile(rows_per_batch, tl)
    per_b = rows_per_batch // tl
    return pl.pallas_call(
        functools.partial(_moe_combine_kernel, tl=tl),
        grid_spec=pltpu.PrefetchScalarGridSpec(
            num_scalar_prefetch=2, grid=(T // tl,),
            in_specs=[pl.BlockSpec(memory_space=pl.ANY),
                      pl.BlockSpec((tl, LANES), lambda i, d1, d2: (i, 0)),
                      pl.BlockSpec((tl, D), lambda i, d1, d2: (i, 0)),
                      pl.BlockSpec((None, None, 1, D), lambda i, d1, d2: (i // per_b, gate_idx, 0, 0))],
            out_specs=pl.BlockSpec((tl, D), lambda i, d1, d2: (i, 0)),
            scratch_shapes=[pltpu.VMEM((2, tl, D), F32), pltpu.SemaphoreType.DMA(())]),
        out_shape=jax.ShapeDtypeStruct((T, D), F32),
        compiler_params=_params(("arbitrary",)), name="moe_combine",
    )(d1, d2, ys, route, res, gate)


def moe_ffn(h, route, counts, w_in, w_out, res, gate, gate_idx, rows_per_batch):
    T = h.shape[0]
    n_experts = w_in.shape[0]
    tm = _tile(T, MOE_ROW_TILE)
    d1, d2, src, tile_e, n_valid = moe_plan(route, counts, n_experts, tm)
    hs = moe_up(h, w_in, src, tile_e, n_valid, tm=tm)
    ys = moe_down(hs, w_out, tile_e, n_valid, tm=tm)
    return moe_combine(ys, route, d1, d2, res, gate, gate_idx, rows_per_batch)


def pool_linear(a, w_pool, colscale, res, gate, gate_idx, rows_per_batch, *, tm=1024, tn=1024,
                name="pool_linear"):
    M, D = a.shape
    NG, Dg, _ = w_pool.shape
    tm, tn = _tile(rows_per_batch, tm), _tile(Dg, tn)
    per_g = Dg // tn
    return _mm_call(a, [w_pool],
                    [pl.BlockSpec((None, Dg, tn), lambda i, j, k: (j // per_g, 0, j % per_g))], D,
                    tm=tm, tn=tn, nk=1,
                    a_spec=pl.BlockSpec((tm, Dg), lambda i, j, k: (i, j // per_g)),
                    act=None, out_dtype=F32, res=res, gate=gate, gate_idx=gate_idx,
                    rows_per_batch=rows_per_batch, colscale=colscale, name=name)


def _s5_tables(a_re, a_im, log_dt, b_re, b_im, c_re, c_im):
    C = S5_CHUNK
    G, P = a_re.shape[1], a_re.shape[2]

    def per_dir(dr):
        ar, ai = a_re[dr].astype(F32), a_im[dr].astype(F32)
        dt = jnp.exp(log_dt[dr].astype(F32))[:, None]
        mag = jnp.exp(ar * dt)
        abar_r, abar_i = mag * jnp.cos(ai * dt), mag * jnp.sin(ai * dt)
        den = ar * ar + ai * ai
        zr = ((abar_r - 1.0) * ar + abar_i * ai) / den
        zi = (abar_i * ar - (abar_r - 1.0) * ai) / den
        br, bi = b_re[dr].astype(F32), b_im[dr].astype(F32)
        bbar_r = zr[..., None] * br - zi[..., None] * bi
        bbar_i = zr[..., None] * bi + zi[..., None] * br

        def power(n):
            n = jnp.asarray(n, F32)[..., None, None]
            m = jnp.exp(n * (ar * dt))
            th = n * (ai * dt)
            return m * jnp.cos(th), m * jnp.sin(th)

        cr, ci = c_re[dr].astype(F32), c_im[dr].astype(F32)
        pr, pi_ = power(jnp.arange(C))
        cz_r = cr[None] * pr[:, :, None, :] - ci[None] * pi_[:, :, None, :]
        cz_i = cr[None] * pi_[:, :, None, :] + ci[None] * pr[:, :, None, :]
        kn = (jnp.einsum("ngip,gpj->gnij", cz_r, bbar_r, precision=HIGHEST)
              - jnp.einsum("ngip,gpj->gnij", cz_i, bbar_i, precision=HIGHEST))
        s_idx = jnp.arange(C)[:, None]
        t_idx = jnp.arange(C)[None, :]
        lag = (t_idx - s_idx) if dr == 0 else (s_idx - t_idx)
        valid = lag >= 0
        kt = kn[:, jnp.clip(lag, 0, C - 1)]
        kt = jnp.where(valid[None, :, :, None, None], kt, 0.0)
        tmat = kt.transpose(0, 1, 4, 2, 3).reshape(G, C * 16, C * 16)
        e = (C - 1 - jnp.arange(C)) if dr == 0 else jnp.arange(C)
        wr_, wi_ = power(e)
        w_re = wr_[..., None] * bbar_r[None] - wi_[..., None] * bbar_i[None]
        w_im = wr_[..., None] * bbar_i[None] + wi_[..., None] * bbar_r[None]
        wv = jnp.concatenate([w_re, w_im], axis=2)
        wv = wv.transpose(1, 0, 3, 2).reshape(G, C * 16, 2 * P)
        e = (jnp.arange(C) + 1) if dr == 0 else (C - jnp.arange(C))
        orr, oi = power(e)
        o_re = cr[None] * orr[:, :, None, :] - ci[None] * oi[:, :, None, :]
        o_im = -(cr[None] * oi[:, :, None, :] + ci[None] * orr[:, :, None, :])
        om = jnp.concatenate([o_re, o_im], axis=3)
        om = om.transpose(1, 3, 0, 2).reshape(G, 2 * P, C * 16)
        return tmat, wv, om

    t0, w0, o0 = per_dir(0)
    t1, w1, o1 = per_dir(1)
    tmat = t0 + t1
    wv = jnp.concatenate([w0, w1], axis=2)
    om = jnp.concatenate([o0, o1], axis=1)
    return tmat.astype(BF16), wv.astype(BF16), om.astype(BF16)


def _s5_scan_tables(a_re, a_im, log_dt, nlev):
    outs_r, outs_i = [], []
    for dr in range(2):
        ar, ai = a_re[dr].astype(F32), a_im[dr].astype(F32)
        dt = jnp.exp(log_dt[dr].astype(F32))[:, None]
        n = (S5_CHUNK * (2.0 ** jnp.arange(nlev, dtype=F32)))[:, None, None]
        m = jnp.exp(n * (ar * dt)[None])
        th = n * (ai * dt)[None]
        zr, zi = m * jnp.cos(th), m * jnp.sin(th)
        outs_r.append(jnp.concatenate([zr, zr], axis=-1))
        outs_i.append(jnp.concatenate([-zi, zi], axis=-1))
    zr = jnp.concatenate(outs_r, axis=-1).transpose(1, 0, 2)
    zi = jnp.concatenate(outs_i, axis=-1).transpose(1, 0, 2)
    return zr, zi


def _s5_kernel(x_ref, wv_ref, tm_ref, om_ref, zr_ref, zi_ref, d_ref, z_ref, *, bsz, nc, nlev, p2):
    x = x_ref[...]
    v = jnp.dot(x, wv_ref[...], preferred_element_type=F32)
    row = lax.broadcasted_iota(jnp.int32, (nc, p2), 0)
    states = []
    for b in range(bsz):
        vb = v[b * nc:(b + 1) * nc]
        pf, pb = vb[:, :p2], vb[:, p2:]
        for k in range(nlev):
            d = 1 << k
            if d >= nc:
                break
            zr = zr_ref[k:k + 1, :]
            zi = zi_ref[k:k + 1, :]
            sf = jnp.where(row >= d, pltpu.roll(pf, d, 0), 0.0)
            pf = pf + zr[:, :p2] * sf + zi[:, :p2] * pltpu.roll(sf, p2 // 2, 1)
            sb = jnp.where(row < nc - d, pltpu.roll(pb, nc - d, 0), 0.0)
            pb = pb + zr[:, p2:] * sb + zi[:, p2:] * pltpu.roll(sb, p2 // 2, 1)
        sf = jnp.where(row >= 1, pltpu.roll(pf, 1, 0), 0.0)
        sb = jnp.where(row < nc - 1, pltpu.roll(pb, nc - 1, 0), 0.0)
        states.append(jnp.concatenate([sf, sb], axis=1))
    s = jnp.concatenate(states, axis=0).astype(BF16)
    y = jnp.dot(x, tm_ref[...], preferred_element_type=F32)
    y = y + jnp.dot(s, om_ref[...], preferred_element_type=F32)
    y = y + x.astype(F32) * d_ref[...]
    z_ref[...] = jax.nn.gelu(y, approximate=True).astype(z_ref.dtype)


def s5_mix(h, tables, scan_tables, d_skip):
    bsz, L, D = h.shape
    C = S5_CHUNK
    G = D // S5_GROUP
    nc = L // C
    tmat, wv, om = tables
    zr, zi = scan_tables
    nlev = zr.shape[1]
    p2 = wv.shape[2] // 2
    cw = C * S5_GROUP
    xg = h.reshape(bsz, nc, C, G, S5_GROUP).transpose(3, 0, 1, 2, 4).reshape(G, bsz * nc, cw)
    dg = jnp.tile(d_skip.astype(F32).reshape(G, 1, S5_GROUP), (1, 1, C))
    z = pl.pallas_call(
        functools.partial(_s5_kernel, bsz=bsz, nc=nc, nlev=nlev, p2=p2),
        grid=(G,),
        in_specs=[pl.BlockSpec((None, bsz * nc, cw), lambda g: (g, 0, 0)),
                  pl.BlockSpec((None, cw, 2 * p2), lambda g: (g, 0, 0)),
                  pl.BlockSpec((None, cw, cw), lambda g: (g, 0, 0)),
                  pl.BlockSpec((None, 2 * p2, cw), lambda g: (g, 0, 0)),
                  pl.BlockSpec((None, nlev, 2 * p2), lambda g: (g, 0, 0)),
                  pl.BlockSpec((None, nlev, 2 * p2), lambda g: (g, 0, 0)),
                  pl.BlockSpec((None, 1, cw), lambda g: (g, 0, 0))],
        out_specs=pl.BlockSpec((None, bsz * nc, cw), lambda g: (g, 0, 0)),
        out_shape=jax.ShapeDtypeStruct((G, bsz * nc, cw), BF16),
        compiler_params=_params(("parallel",)),
        name="s5_mix",
    )(xg, wv, tmat, om, zr, zi, dg)
    return z.reshape(G, bsz, nc, C, S5_GROUP).transpose(1, 2, 3, 0, 4).reshape(bsz, L, D)


def _rel_bucket(rel):
    nb = REL_BUCKETS // 2
    max_exact = nb // 2
    n = jnp.abs(rel)
    large = max_exact + (jnp.log(jnp.maximum(n, 1).astype(F32) / max_exact)
                         / math.log(REL_MAX_DIST / max_exact) * (nb - max_exact)).astype(jnp.int32)
    large = jnp.minimum(large, nb - 1)
    return jnp.where(rel > 0, nb, 0) + jnp.where(n < max_exact, n, large)


def _bias_tiles_kernel(table_ref, bucket_ref, o_ref):
    h = pl.program_id(0)
    bk = bucket_ref[...]
    acc = jnp.zeros(bk.shape, F32)
    for b in range(REL_BUCKETS):
        acc = jnp.where(bk == b, table_ref[b, h], acc)
    o_ref[...] = acc * LOG2E


def bias_tiles(rel_bias, t):
    assert t >= REL_MAX_DIST
    H = rel_bias.shape[1]
    off = jnp.arange(-2, 3, dtype=jnp.int32)[:, None, None] * t
    rel = off + jnp.arange(t, dtype=jnp.int32)[None, None, :] - jnp.arange(t, dtype=jnp.int32)[None, :, None]
    buckets = _rel_bucket(rel)
    return pl.pallas_call(
        _bias_tiles_kernel, grid=(H, 5),
        in_specs=[pl.BlockSpec(memory_space=pltpu.SMEM),
                  pl.BlockSpec((None, t, t), lambda h, d: (d, 0, 0))],
        out_specs=pl.BlockSpec((None, None, t, t), lambda h, d: (h, d, 0, 0)),
        out_shape=jax.ShapeDtypeStruct((H, 5, t, t), F32),
        compiler_params=_params(("parallel", "parallel")), name="bias_tiles",
    )(rel_bias.astype(F32), buckets)


def _lane_tile(x, n):
    return jnp.concatenate([x] * n, axis=1)


def _attn_kernel(q_ref, k_ref, v_ref, bias_ref, lam_ref, g_ref, o_ref, m_ref, l_ref, acc_ref, *,
                 nkv, lambda_init):
    j = pl.program_id(3)

    @pl.when(j == 0)
    def _():
        m_ref[...] = jnp.full(m_ref.shape, -jnp.inf, F32)
        l_ref[...] = jnp.zeros(l_ref.shape, F32)
        acc_ref[...] = jnp.zeros(acc_ref.shape, F32)

    dh = DA_HEAD_DIM
    tq, tk = q_ref.shape[0], k_ref.shape[0]
    rep = v_ref.shape[1] // LANES

    def rows_block(r, carry):
        rows = pl.ds(pl.multiple_of(r * ATTN_ROWS, ATTN_ROWS), ATTN_ROWS)
        q = q_ref[rows, :]
        bias = bias_ref[rows, :]
        v = v_ref[...]
        for m in range(2):
            s = lax.dot_general(q[:, m * dh:(m + 1) * dh], k_ref[:, m * dh:(m + 1) * dh],
                                (((1,), (1,)), ((), ())), preferred_element_type=F32) + bias
            m_prev = m_ref[m, rows, :]
            m_new = jnp.maximum(m_prev, jnp.max(s, axis=-1, keepdims=True))
            alpha = jnp.exp2(m_prev - m_new)
            p = jnp.exp2(s - _lane_tile(m_new, tk // LANES))
            l_ref[m, rows, :] = alpha * l_ref[m, rows, :] + jnp.sum(p, axis=-1, keepdims=True)
            acc_ref[m, rows, :] = (_lane_tile(alpha, rep) * acc_ref[m, rows, :]
                                   + jnp.dot(p.astype(v.dtype), v, preferred_element_type=F32))
            m_ref[m, rows, :] = m_new
        return carry

    lax.fori_loop(0, tq // ATTN_ROWS, rows_block, 0, unroll=ATTN_UNROLL)

    @pl.when(j == nkv - 1)
    def _():
        lam_p = lam_ref[...]
        lam = (jnp.exp(jnp.sum(lam_p[0:1] * lam_p[1:2], axis=-1, keepdims=True))
               - jnp.exp(jnp.sum(lam_p[2:3] * lam_p[3:4], axis=-1, keepdims=True)) + lambda_init)
        o = (acc_ref[0] / _lane_tile(l_ref[0], rep) - lam * (acc_ref[1] / _lane_tile(l_ref[1], rep)))
        o = o * lax.rsqrt(jnp.mean(o * o, axis=-1, keepdims=True) + EPS) * g_ref[...]
        o_ref[...] = (o * (1.0 - lambda_init)).astype(o_ref.dtype)


def diff_attention_core(qkv, bias, lam_params, subln_g, lambda_init, t):
    bsz, L, D3 = qkv.shape
    D = D3 // 3
    hw = 2 * DA_HEAD_DIM
    H = D // hw
    nq = L // t
    return pl.pallas_call(
        functools.partial(_attn_kernel, nkv=nq, lambda_init=lambda_init),
        grid=(bsz, H, nq, nq),
        in_specs=[pl.BlockSpec((None, t, hw), lambda b, h, i, j: (b, i, h)),
                  pl.BlockSpec((None, t, hw), lambda b, h, i, j: (b, j, H + h)),
                  pl.BlockSpec((None, t, hw), lambda b, h, i, j: (b, j, 2 * H + h)),
                  pl.BlockSpec((None, None, t, t),
                               lambda b, h, i, j: (h, jnp.clip(j - i, -2, 2) + 2, 0, 0)),
                  pl.BlockSpec((4, DA_HEAD_DIM), lambda b, h, i, j: (0, 0)),
                  pl.BlockSpec((1, hw), lambda b, h, i, j: (0, 0))],
        out_specs=pl.BlockSpec((None, t, hw), lambda b, h, i, j: (b, i, h)),
        out_shape=jax.ShapeDtypeStruct((bsz, L, D), BF16),
        scratch_shapes=[pltpu.VMEM((2, t, LANES), F32), pltpu.VMEM((2, t, LANES), F32),
                        pltpu.VMEM((2, t, hw), F32)],
        compiler_params=_params(("parallel", "parallel", "parallel", "arbitrary")),
        name="diff_attention",
    )(qkv, qkv, qkv, bias, lam_params, subln_g.reshape(1, hw).astype(F32))


def _hgrn_kernel(q_ref, f_ref, i_ref, gate_ref, lb_ref, ng_ref, tri_ref, o_ref, ofw_ref, s_ref, *,
                 tl, n_tiles, hb):
    dr = pl.program_id(2)
    t = pl.program_id(3)
    C = HG_CHUNK
    nch = tl // C
    W = hb * HG_DIM
    fwd = dr == 0

    @pl.when(t == 0)
    def _():
        s_ref[...] = jnp.zeros(s_ref.shape, F32)

    lb = lb_ref[...]
    sig = jax.nn.sigmoid(f_ref[...])
    logf = jnp.log(lb + (1.0 - lb) * sig)
    kk = (1.0 - lb) * (1.0 - sig)
    hi = logf.astype(BF16)
    lo = (logf - hi.astype(F32)).astype(BF16)
    tri = tri_ref[...]
    g = (jnp.dot(tri, hi, preferred_element_type=F32) + jnp.dot(tri, lo, preferred_element_type=F32))
    g3 = g.reshape(nch, C, W)
    g_tot = jnp.sum(logf.reshape(nch, C, W), axis=1, keepdims=True)
    qt = (q_ref[...] * jnp.exp(g)).astype(BF16)
    kt = (kk * jnp.exp(-g)).astype(BF16)
    kd = (kk.reshape(nch, C, W) * jnp.exp(g_tot - g3)).reshape(tl, W).astype(BF16)
    dec = jnp.exp(g_tot).reshape(nch, W)
    inp = i_ref[...].astype(BF16)
    row = lax.broadcasted_iota(jnp.int32, (C, C), 0)
    col = lax.broadcasted_iota(jnp.int32, (C, C), 1)
    causal = jnp.where(fwd, row - col, col - row) >= 0
    tile_idx = jnp.where(fwd, t, n_tiles - 1 - t)
    base = pl.multiple_of(tile_idx * tl, C)
    nt_dims = (((1,), (1,)), ((), ()))
    tn_dims = (((0,), (0,)), ((), ()))

    def run(order, is_fwd):
        for c in order:
            r = slice(c * C, (c + 1) * C)
            for hh in range(hb):
                cs = slice(hh * HG_DIM, (hh + 1) * HG_DIM)
                qc, kc, ic, kdc = qt[r, cs], kt[r, cs], inp[r, cs], kd[r, cs]
                att = lax.dot_general(qc, kc, nt_dims, preferred_element_type=F32)
                att = jnp.where(causal, att, 0.0).astype(BF16)
                st = s_ref[hh]
                o = (jnp.dot(att, ic, preferred_element_type=F32)
                     + lax.dot_general(qc, st.astype(BF16), nt_dims, preferred_element_type=F32))
                s_ref[hh] = (dec[c:c + 1, cs] * st
                             + lax.dot_general(ic, kdc, tn_dims, preferred_element_type=F32))
                rows = pl.ds(pl.multiple_of(base + c * C, C), C)
                if is_fwd:
                    ofw_ref[rows, cs] = o
                else:
                    tot = ofw_ref[rows, cs] + o
                    y = tot * lax.rsqrt(jnp.mean(tot * tot, axis=-1, keepdims=True) + EPS) * ng_ref[...]
                    gt = gate_ref[r, cs]
                    o_ref[r, cs] = (y * (gt * jax.nn.sigmoid(gt))).astype(o_ref.dtype)

    @pl.when(fwd)
    def _():
        run(range(nch), True)

    @pl.when(jnp.logical_not(fwd))
    def _():
        run(range(nch - 1, -1, -1), False)


def hgrn2_core(proj, lb, norm_g, *, tl=512, hb=HG_HEADS_PER_STEP):
    bsz, L, D5 = proj.shape
    D = D5 // 5
    H = D // HG_DIM
    hb = min(hb, H)
    HB = H // hb
    W = hb * HG_DIM
    tl = _tile(L, tl)
    nt = L // tl
    C = HG_CHUNK
    idx = jnp.arange(tl, dtype=jnp.int32)
    same = (idx[:, None] // C) == (idx[None, :] // C)
    tri = jnp.stack([same & (idx[None, :] <= idx[:, None]),
                     same & (idx[None, :] >= idx[:, None])]).astype(BF16)

    def tile_of(dr, t):
        return jnp.where(dr == 0, t, nt - 1 - t)

    def in_spec(section):
        return pl.BlockSpec((None, tl, W), lambda b, h, dr, t: (b, tile_of(dr, t), section * HB + h))

    f_spec = pl.BlockSpec((None, tl, W), lambda b, h, dr, t: (b, tile_of(dr, t), (1 + dr) * HB + h))
    out_spec = pl.BlockSpec((None, tl, W),
                            lambda b, h, dr, t: (b, jnp.where(dr == 0, nt - 1, nt - 1 - t), h))
    return pl.pallas_call(
        functools.partial(_hgrn_kernel, tl=tl, n_tiles=nt, hb=hb),
        grid=(bsz, HB, 2, nt),
        in_specs=[in_spec(0), f_spec, in_spec(3), in_spec(4),
                  pl.BlockSpec((1, W), lambda b, h, dr, t: (0, h)),
                  pl.BlockSpec((1, HG_DIM), lambda b, h, dr, t: (0, 0)),
                  pl.BlockSpec((None, tl, tl), lambda b, h, dr, t: (dr, 0, 0))],
        out_specs=out_spec,
        out_shape=jax.ShapeDtypeStruct((bsz, L, D), BF16),
        scratch_shapes=[pltpu.VMEM((L, W), F32), pltpu.VMEM((hb, HG_DIM, HG_DIM), F32)],
        compiler_params=_params(("parallel", "parallel", "arbitrary", "arbitrary")),
        name="hgrn2",
    )(proj, proj, proj, proj, lb.reshape(1, D).astype(F32), norm_g.reshape(1, HG_DIM).astype(F32), tri)


def _pool_kernel(h_ref, o_ref, pad_ref, *, L):
    grp = pl.program_id(1)
    x = h_ref[...].astype(F32)
    zeros = jnp.zeros((POOL_HALO, x.shape[1]), F32)
    pad_ref[0:POOL_HALO, :] = zeros
    pad_ref[POOL_HALO:POOL_HALO + L, :] = x
    pad_ref[POOL_HALO + L:POOL_HALO + L + POOL_HALO, :] = zeros
    pos = lax.broadcasted_iota(jnp.int32, (L, 1), 0)
    for gi, w in enumerate(POOL_WINDOWS):
        @pl.when(grp == gi)
        def _(w=w):
            acc = jnp.zeros(x.shape, F32)
            for d in range(-(w // 2), w - w // 2):
                acc = acc + pad_ref[POOL_HALO + d:POOL_HALO + d + L, :]
            lo = jnp.clip(pos - w // 2, 0, L)
            hi = jnp.clip(pos + w - w // 2, 0, L)
            cnt = (hi - lo).astype(F32)
            o_ref[...] = (acc / cnt - x).astype(o_ref.dtype)


def pool_core(h, *, width=128):
    bsz, L, D = h.shape
    ng = len(POOL_WINDOWS)
    dg = D // ng
    width = _tile(dg, width)
    per_g = dg // width
    spec = pl.BlockSpec((None, L, width), lambda b, g, s: (b, 0, g * per_g + s))
    return pl.pallas_call(
        functools.partial(_pool_kernel, L=L),
        grid=(bsz, ng, per_g),
        in_specs=[spec], out_specs=spec,
        out_shape=jax.ShapeDtypeStruct((bsz, L, D), BF16),
        scratch_shapes=[pltpu.VMEM((L + 2 * POOL_HALO, width), F32)],
        compiler_params=_params(("parallel", "parallel", "parallel")),
        name="pool",
    )(h)


def _trunk(x, c, p, shared):
    bsz, L, D = x.shape
    T = bsz * L
    depth = p["norm1_g"].shape[0]
    c_low = small_matmul(c, p["w_ada_down"], silu_in=True)
    for l in range(depth):
        mod = small_matmul(c_low, p["w_ada"][l], p["b_ada"][l]).reshape(bsz, 6, 1, D)
        h = norm_mod(x, p["norm1_g"][l], mod, 1, 0)
        kind, j = l % 4, l // 4
        x2 = x.reshape(T, D)
        if kind == 0:
            z = s5_mix(h, shared["s5_tables"][j], shared["s5_scan"][(j, L)], p["s5_d"][j])
            x = glu_linear(z.reshape(T, D), shared["s5_w_glu"][j], "sigglu", out_dtype=F32,
                           res=x2, gate=mod, gate_idx=2, rows_per_batch=L, name="s5_glu")
        elif kind == 1:
            lambda_init = 0.8 - 0.6 * math.exp(-0.3 * l)
            qkv = linear(h.reshape(T, D), shared["da_w_qkv"][j], name="da_qkv")
            lam_params = jnp.stack([p["da_lam_q1"][j], p["da_lam_k1"][j],
                                    p["da_lam_q2"][j], p["da_lam_k2"][j]]).astype(F32)
            o = diff_attention_core(qkv.reshape(bsz, L, 3 * D), shared["bias_tiles"], lam_params,
                                    p["da_subln_g"][j], lambda_init, shared["attn_tile"])
            x = linear(o.reshape(T, D), shared["da_w_o"][j], out_dtype=F32, res=x2, gate=mod,
                       gate_idx=2, rows_per_batch=L, name="da_out")
        elif kind == 2:
            proj = linear(h.reshape(T, D), shared["hg_w_in"][j], out_dtype=F32, name="hg_in")
            o = hgrn2_core(proj.reshape(bsz, L, 5 * D), shared["hg_lb"][l], p["hg_norm_g"][j])
            x = linear(o.reshape(T, D), shared["hg_w_o"][j], out_dtype=F32, res=x2, gate=mod,
                       gate_idx=2, rows_per_batch=L, name="hg_out")
        else:
            pooled = pool_core(h)
            x = pool_linear(pooled.reshape(T, D), shared["pool_w"][j], p["pool_scale"][j].astype(F32),
                            x2, mod, 2, L)
        x = x.reshape(bsz, L, D)
        x2 = x.reshape(T, D)
        if l % 2 == 0:
            h = norm_mod(x, p["norm2_g"][l], mod, 4, 3)
            hid = glu_linear(h.reshape(T, D), shared["ff_w_in"][l // 2], "swiglu", name="ff_in")
            x = linear(hid, shared["ff_w_out"][l // 2], tk=_ff_tk(hid.shape[1]), out_dtype=F32,
                       res=x2, gate=mod, gate_idx=5, rows_per_batch=L, name="ff_out")
        else:
            h, route, counts = norm_mod(x, p["norm2_g"][l], mod, 4, 3, w_router=p["moe_router"][l // 2])
            x = moe_ffn(h.reshape(T, D), route.reshape(T, LANES), counts, shared["moe_w_in"][l // 2],
                        shared["moe_w_out"][l // 2], x2, mod, 5, L)
        x = x.reshape(bsz, L, D)
    return final_norm(x, p["final_g"])


def _ff_tk(k):
    return k // 2 if (k // 2) % LANES == 0 else k


def kernel(x_prompt, x_sample, c_prompt, c_sample, norm1_g, norm2_g, final_g, w_ada_down, w_ada, b_ada,
           s5_a_re, s5_a_im, s5_log_dt, s5_b_re, s5_b_im, s5_c_re, s5_c_im, s5_d, s5_w_glu,
           da_w_qkv, da_w_o, da_lam_q1, da_lam_k1, da_lam_q2, da_lam_k2, da_subln_g, rel_bias,
           hg_w_in, hg_w_o, hg_norm_g, hg_lb_logits,
           pool_w, pool_scale,
           ff_w_in, ff_w_out, moe_router, moe_w_in, moe_w_out):
    p = dict(norm1_g=norm1_g, norm2_g=norm2_g, final_g=final_g, w_ada_down=w_ada_down, w_ada=w_ada,
             b_ada=b_ada, s5_d=s5_d, da_lam_q1=da_lam_q1, da_lam_k1=da_lam_k1, da_lam_q2=da_lam_q2,
             da_lam_k2=da_lam_k2, da_subln_g=da_subln_g, hg_norm_g=hg_norm_g, pool_scale=pool_scale,
             moe_router=moe_router)
    seq_lens = sorted({x_prompt.shape[1], x_sample.shape[1]})
    attn_tile = _tile(min(seq_lens), 512)
    lb_cum = jnp.cumsum(jax.nn.softmax(hg_lb_logits.astype(F32), axis=0), axis=0)
    hg_lb = jnp.concatenate([jnp.zeros_like(lb_cum[:1]), lb_cum[:-1]], axis=0)
    s5_scan = {}
    for j in range(s5_a_re.shape[0]):
        for L in seq_lens:
            nlev = max(1, math.ceil(math.log2(L // S5_CHUNK)))
            s5_scan[(j, L)] = _s5_scan_tables(s5_a_re[j], s5_a_im[j], s5_log_dt[j], nlev)
    d_model = x_prompt.shape[-1]
    qkv_scale = jnp.concatenate([jnp.full((d_model,), DA_HEAD_DIM ** -0.5 * LOG2E, F32),
                                 jnp.ones((2 * d_model,), F32)])
    shared = dict(
        s5_tables=[_s5_tables(s5_a_re[j], s5_a_im[j], s5_log_dt[j], s5_b_re[j], s5_b_im[j],
                              s5_c_re[j], s5_c_im[j]) for j in range(s5_a_re.shape[0])],
        s5_scan=s5_scan,
        s5_w_glu=s5_w_glu.astype(BF16), da_w_qkv=(da_w_qkv * qkv_scale).astype(BF16),
        da_w_o=da_w_o.astype(BF16),
        hg_w_in=hg_w_in.astype(BF16), hg_w_o=hg_w_o.astype(BF16), pool_w=pool_w.astype(BF16),
        ff_w_in=ff_w_in.astype(BF16), ff_w_out=ff_w_out.astype(BF16),
        moe_w_in=moe_w_in.astype(BF16), moe_w_out=moe_w_out.astype(BF16),
        hg_lb=hg_lb, attn_tile=attn_tile, bias_tiles=bias_tiles(rel_bias, attn_tile),
    )
    y_prompt = _trunk(x_prompt, c_prompt, p, shared)
    y_sample = _trunk(x_sample, c_sample, p, shared)
    return (y_prompt, y_sample)
```

```python
import functools
import math

import jax
import jax.numpy as jnp
from jax import lax
from jax.experimental import pallas as pl
from jax.experimental.pallas import tpu as pltpu

F32 = jnp.float32
BF16 = jnp.bfloat16
EPS = 1e-6
LANES = 128
V7X_VMEM_LIMIT = 60 * 1024 * 1024

S5_GROUP = 16
S5_CHUNK = 16
DA_HEAD_DIM = 128
ATTN_ROWS = 128
ATTN_UNROLL = 4
REL_BUCKETS = 32
REL_MAX_DIST = 128
HG_DIM = 128
HG_CHUNK = 64
HG_HEADS_PER_STEP = 4
HG_SAFE_LOG_DECAY = -80.0
POOL_WINDOWS = (2, 4, 8, 16)
POOL_HALO = 8
MOE_ROW_TILE = 512
HIGHEST = lax.Precision.HIGHEST
LOG2E = math.log2(math.e)


def _params(sem, vmem=V7X_VMEM_LIMIT):
    return pltpu.CompilerParams(dimension_semantics=sem, vmem_limit_bytes=vmem)


def _tile(n, pref):
    t = min(n, pref)
    while n % t:
        t //= 2
    return t


def _small_mm_kernel(a_ref, w_ref, b_ref, o_ref, *, silu_in):
    a = a_ref[...]
    if silu_in:
        a = a * jax.nn.sigmoid(a)
    o_ref[...] = jnp.dot(a, w_ref[...], preferred_element_type=F32, precision=HIGHEST) + b_ref[...]


def small_matmul(a, w, b=None, *, silu_in=False):
    m, k = a.shape
    n = w.shape[1]
    tn = _tile(n, 2048)
    if b is None:
        b = jnp.zeros((1, n), F32)
    return pl.pallas_call(
        functools.partial(_small_mm_kernel, silu_in=silu_in),
        grid=(n // tn,),
        in_specs=[pl.BlockSpec((m, k), lambda j: (0, 0)),
                  pl.BlockSpec((k, tn), lambda j: (0, j)),
                  pl.BlockSpec((1, tn), lambda j: (0, j))],
        out_specs=pl.BlockSpec((m, tn), lambda j: (0, j)),
        out_shape=jax.ShapeDtypeStruct((m, n), F32),
        compiler_params=_params(("parallel",)),
        name="small_matmul",
    )(a, w, b.reshape(1, n))


def _norm_mod_kernel(x_ref, g_ref, sc_ref, sh_ref, o_ref):
    x = x_ref[...]
    y = x * lax.rsqrt(jnp.mean(x * x, axis=-1, keepdims=True) + EPS) * g_ref[...]
    o_ref[...] = (y * (1.0 + sc_ref[...]) + sh_ref[...]).astype(o_ref.dtype)


ROUTE_E1, ROUTE_E2, ROUTE_R1, ROUTE_R2, ROUTE_G1, ROUTE_G2 = range(6)


def _norm_mod_router_kernel(x_ref, g_ref, sc_ref, sh_ref, wr_ref, tri_ref, o_ref, route_ref, cnt_ref,
                            run_ref, *, n_experts):
    @pl.when((pl.program_id(0) == 0) & (pl.program_id(1) == 0))
    def _():
        run_ref[...] = jnp.zeros(run_ref.shape, F32)

    x = x_ref[...]
    y = x * lax.rsqrt(jnp.mean(x * x, axis=-1, keepdims=True) + EPS) * g_ref[...]
    h = y * (1.0 + sc_ref[...]) + sh_ref[...]
    o_ref[...] = h.astype(o_ref.dtype)
    logits = jnp.dot(h, wr_ref[...], preferred_element_type=F32, precision=HIGHEST)
    lane = lax.broadcasted_iota(jnp.int32, logits.shape, 1)
    neg = jnp.float32(-jnp.inf)
    logits = jnp.where(lane < n_experts, logits, neg)
    v1 = jnp.max(logits, axis=-1, keepdims=True)
    i1 = jnp.min(jnp.where(logits == v1, lane, LANES), axis=-1, keepdims=True)
    rest = jnp.where(lane == i1, neg, logits)
    v2 = jnp.max(rest, axis=-1, keepdims=True)
    i2 = jnp.min(jnp.where(rest == v2, lane, LANES), axis=-1, keepdims=True)
    e2 = jnp.exp(v2 - v1)
    g1 = 1.0 / (1.0 + e2)
    g2 = e2 / (1.0 + e2)
    sel = jnp.where((lane == i1) | (lane == i2), 1.0, 0.0)
    before = run_ref[...] + jnp.dot(tri_ref[...], sel.astype(BF16), preferred_element_type=F32)
    r1 = jnp.sum(jnp.where(lane == i1, before, 0.0), axis=-1, keepdims=True)
    r2 = jnp.sum(jnp.where(lane == i2, before, 0.0), axis=-1, keepdims=True)
    run_ref[...] = run_ref[...] + jnp.sum(sel, axis=0, keepdims=True)
    cnt_ref[...] = run_ref[...]
    rec = jnp.zeros(logits.shape, F32)
    for k, val in ((ROUTE_E1, i1.astype(F32)), (ROUTE_E2, i2.astype(F32)), (ROUTE_R1, r1),
                   (ROUTE_R2, r2), (ROUTE_G1, g1), (ROUTE_G2, g2)):
        rec = jnp.where(lane == k, val, rec)
    route_ref[...] = rec


def norm_mod(x, g, mod, sc_idx, sh_idx, w_router=None):
    bsz, L, D = x.shape
    tl = _tile(L, 256)
    grid = (bsz, L // tl)
    x_spec = pl.BlockSpec((None, tl, D), lambda b, i: (b, i, 0))
    g_spec = pl.BlockSpec((1, D), lambda b, i: (0, 0))
    sc_spec = pl.BlockSpec((None, None, 1, D), lambda b, i: (b, sc_idx, 0, 0))
    sh_spec = pl.BlockSpec((None, None, 1, D), lambda b, i: (b, sh_idx, 0, 0))
    h_shape = jax.ShapeDtypeStruct((bsz, L, D), BF16)
    if w_router is None:
        return pl.pallas_call(
            _norm_mod_kernel, grid=grid,
            in_specs=[x_spec, g_spec, sc_spec, sh_spec],
            out_specs=x_spec, out_shape=h_shape,
            compiler_params=_params(("parallel", "parallel")), name="norm_mod",
        )(x, g.reshape(1, D), mod, mod)
    n_experts = w_router.shape[1]
    wr = jnp.zeros((D, LANES), F32).at[:, :n_experts].set(w_router)
    idx = jnp.arange(tl, dtype=jnp.int32)
    tri = (idx[None, :] < idx[:, None]).astype(BF16)
    return pl.pallas_call(
        functools.partial(_norm_mod_router_kernel, n_experts=n_experts), grid=grid,
        in_specs=[x_spec, g_spec, sc_spec, sh_spec, pl.BlockSpec((D, LANES), lambda b, i: (0, 0)),
                  pl.BlockSpec((tl, tl), lambda b, i: (0, 0))],
        out_specs=[x_spec, pl.BlockSpec((None, tl, LANES), lambda b, i: (b, i, 0)),
                   pl.BlockSpec((1, LANES), lambda b, i: (0, 0))],
        out_shape=[jax.ShapeDtypeStruct((bsz, L, D), F32), jax.ShapeDtypeStruct((bsz, L, LANES), F32),
                   jax.ShapeDtypeStruct((1, LANES), F32)],
        scratch_shapes=[pltpu.VMEM((1, LANES), F32)],
        compiler_params=_params(("arbitrary", "arbitrary")), name="norm_mod_router",
    )(x, g.reshape(1, D), mod, mod, wr, tri)


def _final_norm_kernel(x_ref, g_ref, o_ref):
    x = x_ref[...]
    o_ref[...] = x * lax.rsqrt(jnp.mean(x * x, axis=-1, keepdims=True) + EPS) * g_ref[...]


def final_norm(x, g):
    bsz, L, D = x.shape
    tl = _tile(L, 256)
    spec = pl.BlockSpec((None, tl, D), lambda b, i: (b, i, 0))
    return pl.pallas_call(
        _final_norm_kernel, grid=(bsz, L // tl),
        in_specs=[spec, pl.BlockSpec((1, D), lambda b, i: (0, 0))],
        out_specs=spec, out_shape=jax.ShapeDtypeStruct((bsz, L, D), F32),
        compiler_params=_params(("parallel", "parallel")), name="final_norm",
    )(x, g.reshape(1, D))


def _mm_kernel(*refs, act, nk, has_res, has_colscale):
    it = iter(refs)
    a_ref = next(it)
    w1_ref = next(it)
    w2_ref = next(it) if act else None
    res_ref = next(it) if has_res else None
    gate_ref = next(it) if has_res else None
    cs_ref = next(it) if has_colscale else None
    o_ref = next(it)
    acc1_ref = next(it) if nk > 1 else None
    acc2_ref = next(it) if (nk > 1 and act) else None
    k = pl.program_id(2)

    a = a_ref[...]
    p1 = jnp.dot(a, w1_ref[...], preferred_element_type=F32)
    p2 = jnp.dot(a, w2_ref[...], preferred_element_type=F32) if act else None

    def finish(y1, y2):
        if act == "swiglu":
            y = y1 * jax.nn.sigmoid(y1) * y2
        elif act == "sigglu":
            y = y1 * jax.nn.sigmoid(y2)
        else:
            y = y1
        if has_colscale:
            y = y * cs_ref[...]
        if has_res:
            y = res_ref[...] + gate_ref[...] * y
        o_ref[...] = y.astype(o_ref.dtype)

    if nk == 1:
        finish(p1, p2)
    else:
        @pl.when(k == 0)
        def _():
            acc1_ref[...] = p1
            if act:
                acc2_ref[...] = p2

        @pl.when(k > 0)
        def _():
            acc1_ref[...] += p1
            if act:
                acc2_ref[...] += p2

        @pl.when(k == nk - 1)
        def _():
            finish(acc1_ref[...], acc2_ref[...] if act else None)


def _mm_call(a, weights, w_specs, n_out, *, tm, tn, nk, a_spec, act, out_dtype,
             res, gate, gate_idx, rows_per_batch, colscale, name):
    M = a.shape[0]
    grid = (M // tm, n_out // tn, nk)
    in_specs = [a_spec] + list(w_specs)
    args = [a] + list(weights)
    if res is not None:
        tiles_per_batch = rows_per_batch // tm
        in_specs.append(pl.BlockSpec((tm, tn), lambda i, j, k: (i, j)))
        in_specs.append(pl.BlockSpec((None, None, 1, tn),
                                     lambda i, j, k: (i // tiles_per_batch, gate_idx, 0, j)))
        args += [res, gate]
    if colscale is not None:
        in_specs.append(pl.BlockSpec((1, tn), lambda i, j, k: (0, j)))
        args.append(colscale.reshape(1, n_out))
    scratch = []
    if nk > 1:
        scratch.append(pltpu.VMEM((tm, tn), F32))
        if act:
            scratch.append(pltpu.VMEM((tm, tn), F32))
    return pl.pallas_call(
        functools.partial(_mm_kernel, act=act, nk=nk, has_res=res is not None,
                          has_colscale=colscale is not None),
        grid=grid, in_specs=in_specs,
        out_specs=pl.BlockSpec((tm, tn), lambda i, j, k: (i, j)),
        out_shape=jax.ShapeDtypeStruct((M, n_out), out_dtype),
        scratch_shapes=scratch,
        compiler_params=_params(("parallel", "parallel", "arbitrary")),
        name=name,
    )(*args)


def linear(a, w, *, tm=1024, tn=1024, tk=None, out_dtype=BF16, res=None, gate=None, gate_idx=0,
           rows_per_batch=None, name="linear"):
    M, K = a.shape
    N = w.shape[1]
    tm, tn = _tile(M if rows_per_batch is None else rows_per_batch, tm), _tile(N, tn)
    tk = K if tk is None else tk
    nk = K // tk
    return _mm_call(a, [w], [pl.BlockSpec((tk, tn), lambda i, j, k: (k, j))], N,
                    tm=tm, tn=tn, nk=nk, a_spec=pl.BlockSpec((tm, tk), lambda i, j, k: (i, k)),
                    act=None, out_dtype=out_dtype, res=res, gate=gate, gate_idx=gate_idx,
                    rows_per_batch=rows_per_batch, colscale=None, name=name)


def glu_linear(a, w, act, *, tm=1024, tn=512, out_dtype=BF16, res=None, gate=None, gate_idx=0,
               rows_per_batch=None, name="glu_linear"):
    M, K = a.shape
    F = w.shape[1] // 2
    tm, tn = _tile(M if rows_per_batch is None else rows_per_batch, tm), _tile(F, tn)
    off = F // tn
    specs = [pl.BlockSpec((K, tn), lambda i, j, k: (0, j)),
             pl.BlockSpec((K, tn), lambda i, j, k: (0, j + off))]
    return _mm_call(a, [w, w], specs, F, tm=tm, tn=tn, nk=1,
                    a_spec=pl.BlockSpec((tm, K), lambda i, j, k: (i, 0)),
                    act=act, out_dtype=out_dtype, res=res, gate=gate, gate_idx=gate_idx,
                    rows_per_batch=rows_per_batch, colscale=None, name=name)


def moe_plan(route, counts, n_experts, tm):
    T = route.shape[0]
    e1, e2 = route[:, ROUTE_E1].astype(jnp.int32), route[:, ROUTE_E2].astype(jnp.int32)
    r1, r2 = route[:, ROUTE_R1].astype(jnp.int32), route[:, ROUTE_R2].astype(jnp.int32)
    cnt = counts[0, :n_experts].astype(jnp.int32)
    gsz = (cnt + tm - 1) // tm * tm
    ends = jnp.cumsum(gsz)
    offs = ends - gsz

    def start_of(e):
        return sum(jnp.where(e == k, offs[k], 0) for k in range(n_experts))

    n_tiles = (2 * T) // tm + n_experts
    starts = jnp.arange(n_tiles, dtype=jnp.int32) * tm
    tile_e = jnp.minimum(jnp.sum((starts[:, None] >= ends[None, :]).astype(jnp.int32), axis=1),
                         n_experts - 1)
    n_valid = (ends[-1] // tm).reshape(1)
    d1, d2 = start_of(e1) + r1, start_of(e2) + r2
    tok = jnp.arange(T, dtype=jnp.int32)
    src = jnp.zeros((n_tiles * tm,), jnp.int32).at[d1].set(tok).at[d2].set(tok)
    return d1, d2, src, tile_e, n_valid


def _row_copies(src_at, dst_at, sem, n, start):
    def body(r, carry):
        for cp in (pltpu.make_async_copy(s_, d_, sem) for s_, d_ in zip(src_at(r), dst_at(r))):
            cp.start() if start else cp.wait()
        return carry
    lax.fori_loop(0, n, body, 0)


def _moe_up_kernel(src_ref, te_ref, nv_ref, h_ref, wg_ref, wu_ref, o_ref, a_ref, accg_ref, accu_ref, sem,
                   *, nk, tm, tk):
    del te_ref
    i = pl.program_id(0)
    k = pl.program_id(1)
    valid = i < nv_ref[0]

    @pl.when(valid & (k == 0))
    def _():
        def src_at(r):
            return (h_ref.at[pl.ds(src_ref[i * tm + r], 1), :],)

        def dst_at(r):
            return (a_ref.at[pl.ds(r, 1), :],)

        _row_copies(src_at, dst_at, sem, tm, True)
        _row_copies(src_at, dst_at, sem, tm, False)

    @pl.when(valid)
    def _():
        a = a_ref[:, pl.ds(pl.multiple_of(k * tk, tk), tk)].astype(BF16)
        pg = jnp.dot(a, wg_ref[...], preferred_element_type=F32)
        pu = jnp.dot(a, wu_ref[...], preferred_element_type=F32)

        @pl.when(k == 0)
        def _():
            accg_ref[...] = pg
            accu_ref[...] = pu

        @pl.when(k > 0)
        def _():
            accg_ref[...] += pg
            accu_ref[...] += pu

        @pl.when(k == nk - 1)
        def _():
            y = accg_ref[...]
            o_ref[...] = (y * jax.nn.sigmoid(y) * accu_ref[...]).astype(o_ref.dtype)

    @pl.when(jnp.logical_not(valid) & (k == nk - 1))
    def _():
        o_ref[...] = jnp.zeros(o_ref.shape, o_ref.dtype)


def moe_up(h, w_in, src, tile_e, n_valid, *, tm, tk=2048):
    K = h.shape[1]
    R = src.shape[0]
    F = w_in.shape[2] // 2
    tk = _tile(K, tk)
    nk = K // tk
    return pl.pallas_call(
        functools.partial(_moe_up_kernel, nk=nk, tm=tm, tk=tk),
        grid_spec=pltpu.PrefetchScalarGridSpec(
            num_scalar_prefetch=3, grid=(R // tm, nk),
            in_specs=[pl.BlockSpec(memory_space=pl.ANY),
                      pl.BlockSpec((None, tk, F), lambda i, k, src, te, nv: (te[i], k, 0)),
                      pl.BlockSpec((None, tk, F), lambda i, k, src, te, nv: (te[i], k, 1))],
            out_specs=pl.BlockSpec((tm, F), lambda i, k, src, te, nv: (i, 0)),
            scratch_shapes=[pltpu.VMEM((tm, K), F32), pltpu.VMEM((tm, F), F32), pltpu.VMEM((tm, F), F32),
                            pltpu.SemaphoreType.DMA(())]),
        out_shape=jax.ShapeDtypeStruct((R, F), BF16),
        compiler_params=_params(("arbitrary", "arbitrary")), name="moe_up",
    )(src, tile_e, n_valid, h, w_in, w_in)


def _moe_down_kernel(te_ref, nv_ref, a_ref, w_ref, o_ref):
    del te_ref
    valid = pl.program_id(0) < nv_ref[0]

    @pl.when(valid)
    def _():
        o_ref[...] = jnp.dot(a_ref[...], w_ref[...], preferred_element_type=F32)

    @pl.when(jnp.logical_not(valid))
    def _():
        o_ref[...] = jnp.zeros(o_ref.shape, o_ref.dtype)


def moe_down(hs, w_out, tile_e, n_valid, *, tm, tn=1024):
    R, F = hs.shape
    D = w_out.shape[2]
    tn = _tile(D, tn)
    return pl.pallas_call(
        _moe_down_kernel,
        grid_spec=pltpu.PrefetchScalarGridSpec(
            num_scalar_prefetch=2, grid=(R // tm, D // tn),
            in_specs=[pl.BlockSpec((tm, F), lambda i, j, te, nv: (i, 0)),
                      pl.BlockSpec((None, F, tn), lambda i, j, te, nv: (te[i], 0, j))],
            out_specs=pl.BlockSpec((tm, tn), lambda i, j, te, nv: (i, j))),
        out_shape=jax.ShapeDtypeStruct((R, D), F32),
        compiler_params=_params(("parallel", "parallel")), name="moe_down",
    )(tile_e, n_valid, hs, w_out)


def _moe_combine_kernel(d1_ref, d2_ref, ys_ref, route_ref, res_ref, gate_ref, o_ref, buf_ref, sem, *, tl):
    base = pl.program_id(0) * tl

    def src_at(r):
        return ys_ref.at[pl.ds(d1_ref[base + r], 1), :], ys_ref.at[pl.ds(d2_ref[base + r], 1), :]

    def dst_at(r):
        return buf_ref.at[0, pl.ds(r, 1), :], buf_ref.at[1, pl.ds(r, 1), :]

    _row_copies(src_at, dst_at, sem, tl, True)
    _row_copies(src_at, dst_at, sem, tl, False)
    route = route_ref[...]
    g1 = route[:, ROUTE_G1:ROUTE_G1 + 1]
    g2 = route[:, ROUTE_G2:ROUTE_G2 + 1]
    o_ref[...] = res_ref[...] + gate_ref[...] * (g1 * buf_ref[0] + g2 * buf_ref[1])


def moe_combine(ys, route, d1, d2, res, gate, gate_idx, rows_per_batch, *, tl=256):
    T, D = res.shape
    tl = _tile(rows_per_batch, tl)
    per_b = rows_per_batch // tl
    return pl.pallas_call(
        functools.partial(_moe_combine_kernel, tl=tl),
        grid_spec=pltpu.PrefetchScalarGridSpec(
            num_scalar_prefetch=2, grid=(T // tl,),
            in_specs=[pl.BlockSpec(memory_space=pl.ANY),
                      pl.BlockSpec((tl, LANES), lambda i, d1, d2: (i, 0)),
                      pl.BlockSpec((tl, D), lambda i, d1, d2: (i, 0)),
                      pl.BlockSpec((None, None, 1, D), lambda i, d1, d2: (i // per_b, gate_idx, 0, 0))],
            out_specs=pl.BlockSpec((tl, D), lambda i, d1, d2: (i, 0)),
            scratch_shapes=[pltpu.VMEM((2, tl, D), F32), pltpu.SemaphoreType.DMA(())]),
        out_shape=jax.ShapeDtypeStruct((T, D), F32),
        compiler_params=_params(("arbitrary",)), name="moe_combine",
    )(d1, d2, ys, route, res, gate)


def moe_ffn(h, route, counts, w_in, w_out, res, gate, gate_idx, rows_per_batch):
    T = h.shape[0]
    n_experts = w_in.shape[0]
    tm = _tile(T, MOE_ROW_TILE)
    d1, d2, src, tile_e, n_valid = moe_plan(route, counts, n_experts, tm)
    hs = moe_up(h, w_in, src, tile_e, n_valid, tm=tm)
    ys = moe_down(hs, w_out, tile_e, n_valid, tm=tm)
    return moe_combine(ys, route, d1, d2, res, gate, gate_idx, rows_per_batch)


def pool_linear(a, w_pool, colscale, res, gate, gate_idx, rows_per_batch, *, tm=1024, tn=1024,
                name="pool_linear"):
    M, D = a.shape
    NG, Dg, _ = w_pool.shape
    tm, tn = _tile(rows_per_batch, tm), _tile(Dg, tn)
    per_g = Dg // tn
    return _mm_call(a, [w_pool],
                    [pl.BlockSpec((None, Dg, tn), lambda i, j, k: (j // per_g, 0, j % per_g))], D,
                    tm=tm, tn=tn, nk=1,
                    a_spec=pl.BlockSpec((tm, Dg), lambda i, j, k: (i, j // per_g)),
                    act=None, out_dtype=F32, res=res, gate=gate, gate_idx=gate_idx,
                    rows_per_batch=rows_per_batch, colscale=colscale, name=name)


def _s5_tables(a_re, a_im, log_dt, b_re, b_im, c_re, c_im):
    C = S5_CHUNK
    G, P = a_re.shape[1], a_re.shape[2]

    def per_dir(dr):
        ar, ai = a_re[dr].astype(F32), a_im[dr].astype(F32)
        dt = jnp.exp(log_dt[dr].astype(F32))[:, None]
        mag = jnp.exp(ar * dt)
        abar_r, abar_i = mag * jnp.cos(ai * dt), mag * jnp.sin(ai * dt)
        den = ar * ar + ai * ai
        zr = ((abar_r - 1.0) * ar + abar_i * ai) / den
        zi = (abar_i * ar - (abar_r - 1.0) * ai) / den
        br, bi = b_re[dr].astype(F32), b_im[dr].astype(F32)
        bbar_r = zr[..., None] * br - zi[..., None] * bi
        bbar_i = zr[..., None] * bi + zi[..., None] * br

        def power(n):
            n = jnp.asarray(n, F32)[..., None, None]
            m = jnp.exp(n * (ar * dt))
            th = n * (ai * dt)
            return m * jnp.cos(th), m * jnp.sin(th)

        cr, ci = c_re[dr].astype(F32), c_im[dr].astype(F32)
        pr, pi_ = power(jnp.arange(C))
        cz_r = cr[None] * pr[:, :, None, :] - ci[None] * pi_[:, :, None, :]
        cz_i = cr[None] * pi_[:, :, None, :] + ci[None] * pr[:, :, None, :]
        kn = (jnp.einsum("ngip,gpj->gnij", cz_r, bbar_r, precision=HIGHEST)
              - jnp.einsum("ngip,gpj->gnij", cz_i, bbar_i, precision=HIGHEST))
        s_idx = jnp.arange(C)[:, None]
        t_idx = jnp.arange(C)[None, :]
        lag = (t_idx - s_idx) if dr == 0 else (s_idx - t_idx)
        valid = lag >= 0
        kt = kn[:, jnp.clip(lag, 0, C - 1)]
        kt = jnp.where(valid[None, :, :, None, None], kt, 0.0)
        tmat = kt.transpose(0, 1, 4, 2, 3).reshape(G, C * 16, C * 16)
        e = (C - 1 - jnp.arange(C)) if dr == 0 else jnp.arange(C)
        wr_, wi_ = power(e)
        w_re = wr_[..., None] * bbar_r[None] - wi_[..., None] * bbar_i[None]
        w_im = wr_[..., None] * bbar_i[None] + wi_[..., None] * bbar_r[None]
        wv = jnp.concatenate([w_re, w_im], axis=2)
        wv = wv.transpose(1, 0, 3, 2).reshape(G, C * 16, 2 * P)
        e = (jnp.arange(C) + 1) if dr == 0 else (C - jnp.arange(C))
        orr, oi = power(e)
        o_re = cr[None] * orr[:, :, None, :] - ci[None] * oi[:, :, None, :]
        o_im = -(cr[None] * oi[:, :, None, :] + ci[None] * orr[:, :, None, :])
        om = jnp.concatenate([o_re, o_im], axis=3)
        om = om.transpose(1, 3, 0, 2).reshape(G, 2 * P, C * 16)
        return tmat, wv, om

    t0, w0, o0 = per_dir(0)
    t1, w1, o1 = per_dir(1)
    tmat = t0 + t1
    wv = jnp.concatenate([w0, w1], axis=2)
    om = jnp.concatenate([o0, o1], axis=1)
    return tmat.astype(BF16), wv.astype(BF16), om.astype(BF16)


def _s5_scan_tables(a_re, a_im, log_dt, nlev):
    outs_r, outs_i = [], []
    for dr in range(2):
        ar, ai = a_re[dr].astype(F32), a_im[dr].astype(F32)
        dt = jnp.exp(log_dt[dr].astype(F32))[:, None]
        n = (S5_CHUNK * (2.0 ** jnp.arange(nlev, dtype=F32)))[:, None, None]
        m = jnp.exp(n * (ar * dt)[None])
        th = n * (ai * dt)[None]
        zr, zi = m * jnp.cos(th), m * jnp.sin(th)
        outs_r.append(jnp.concatenate([zr, zr], axis=-1))
        outs_i.append(jnp.concatenate([-zi, zi], axis=-1))
    zr = jnp.concatenate(outs_r, axis=-1).transpose(1, 0, 2)
    zi = jnp.concatenate(outs_i, axis=-1).transpose(1, 0, 2)
    return zr, zi


def _s5_kernel(x_ref, wv_ref, tm_ref, om_ref, zr_ref, zi_ref, d_ref, z_ref, *, bsz, nc, nlev, p2):
    x = x_ref[...]
    v = jnp.dot(x, wv_ref[...], preferred_element_type=F32)
    row = lax.broadcasted_iota(jnp.int32, (nc, p2), 0)
    states = []
    for b in range(bsz):
        vb = v[b * nc:(b + 1) * nc]
        pf, pb = vb[:, :p2], vb[:, p2:]
        for k in range(nlev):
            d = 1 << k
            if d >= nc:
                break
            zr = zr_ref[k:k + 1, :]
            zi = zi_ref[k:k + 1, :]
            sf = jnp.where(row >= d, pltpu.roll(pf, d, 0), 0.0)
            pf = pf + zr[:, :p2] * sf + zi[:, :p2] * pltpu.roll(sf, p2 // 2, 1)
            sb = jnp.where(row < nc - d, pltpu.roll(pb, nc - d, 0), 0.0)
            pb = pb + zr[:, p2:] * sb + zi[:, p2:] * pltpu.roll(sb, p2 // 2, 1)
        sf = jnp.where(row >= 1, pltpu.roll(pf, 1, 0), 0.0)
        sb = jnp.where(row < nc - 1, pltpu.roll(pb, nc - 1, 0), 0.0)
        states.append(jnp.concatenate([sf, sb], axis=1))
    s = jnp.concatenate(states, axis=0).astype(BF16)
    y = jnp.dot(x, tm_ref[...], preferred_element_type=F32)
    y = y + jnp.dot(s, om_ref[...], preferred_element_type=F32)
    y = y + x.astype(F32) * d_ref[...]
    z_ref[...] = jax.nn.gelu(y, approximate=True).astype(z_ref.dtype)


def s5_mix(h, tables, scan_tables, d_skip):
    bsz, L, D = h.shape
    C = S5_CHUNK
    G = D // S5_GROUP
    nc = L // C
    tmat, wv, om = tables
    zr, zi = scan_tables
    nlev = zr.shape[1]
    p2 = wv.shape[2] // 2
    cw = C * S5_GROUP
    xg = h.reshape(bsz, nc, C, G, S5_GROUP).transpose(3, 0, 1, 2, 4).reshape(G, bsz * nc, cw)
    dg = jnp.tile(d_skip.astype(F32).reshape(G, 1, S5_GROUP), (1, 1, C))
    z = pl.pallas_call(
        functools.partial(_s5_kernel, bsz=bsz, nc=nc, nlev=nlev, p2=p2),
        grid=(G,),
        in_specs=[pl.BlockSpec((None, bsz * nc, cw), lambda g: (g, 0, 0)),
                  pl.BlockSpec((None, cw, 2 * p2), lambda g: (g, 0, 0)),
                  pl.BlockSpec((None, cw, cw), lambda g: (g, 0, 0)),
                  pl.BlockSpec((None, 2 * p2, cw), lambda g: (g, 0, 0)),
                  pl.BlockSpec((None, nlev, 2 * p2), lambda g: (g, 0, 0)),
                  pl.BlockSpec((None, nlev, 2 * p2), lambda g: (g, 0, 0)),
                  pl.BlockSpec((None, 1, cw), lambda g: (g, 0, 0))],
        out_specs=pl.BlockSpec((None, bsz * nc, cw), lambda g: (g, 0, 0)),
        out_shape=jax.ShapeDtypeStruct((G, bsz * nc, cw), BF16),
        compiler_params=_params(("parallel",)),
        name="s5_mix",
    )(xg, wv, tmat, om, zr, zi, dg)
    return z.reshape(G, bsz, nc, C, S5_GROUP).transpose(1, 2, 3, 0, 4).reshape(bsz, L, D)


def _rel_bucket(rel):
    nb = REL_BUCKETS // 2
    max_exact = nb // 2
    n = jnp.abs(rel)
    large = max_exact + (jnp.log(jnp.maximum(n, 1).astype(F32) / max_exact)
                         / math.log(REL_MAX_DIST / max_exact) * (nb - max_exact)).astype(jnp.int32)
    large = jnp.minimum(large, nb - 1)
    return jnp.where(rel > 0, nb, 0) + jnp.where(n < max_exact, n, large)


def _bias_tiles_kernel(table_ref, bucket_ref, o_ref):
    h = pl.program_id(0)
    bk = bucket_ref[...]
    acc = jnp.zeros(bk.shape, F32)
    for b in range(REL_BUCKETS):
        acc = jnp.where(bk == b, table_ref[b, h], acc)
    o_ref[...] = acc * LOG2E


def bias_tiles(rel_bias, t):
    assert t >= REL_MAX_DIST
    H = rel_bias.shape[1]
    off = jnp.arange(-2, 3, dtype=jnp.int32)[:, None, None] * t
    rel = off + jnp.arange(t, dtype=jnp.int32)[None, None, :] - jnp.arange(t, dtype=jnp.int32)[None, :, None]
    buckets = _rel_bucket(rel)
    return pl.pallas_call(
        _bias_tiles_kernel, grid=(H, 5),
        in_specs=[pl.BlockSpec(memory_space=pltpu.SMEM),
                  pl.BlockSpec((None, t, t), lambda h, d: (d, 0, 0))],
        out_specs=pl.BlockSpec((None, None, t, t), lambda h, d: (h, d, 0, 0)),
        out_shape=jax.ShapeDtypeStruct((H, 5, t, t), F32),
        compiler_params=_params(("parallel", "parallel")), name="bias_tiles",
    )(rel_bias.astype(F32), buckets)


def _lane_tile(x, n):
    return jnp.concatenate([x] * n, axis=1)


def _attn_kernel(q_ref, k_ref, v_ref, bias_ref, lam_ref, g_ref, o_ref, m_ref, l_ref, acc_ref, *,
                 nkv, lambda_init):
    j = pl.program_id(3)

    @pl.when(j == 0)
    def _():
        m_ref[...] = jnp.full(m_ref.shape, -jnp.inf, F32)
        l_ref[...] = jnp.zeros(l_ref.shape, F32)
        acc_ref[...] = jnp.zeros(acc_ref.shape, F32)

    dh = DA_HEAD_DIM
    tq, tk = q_ref.shape[0], k_ref.shape[0]
    rep = v_ref.shape[1] // LANES

    def rows_block(r, carry):
        rows = pl.ds(pl.multiple_of(r * ATTN_ROWS, ATTN_ROWS), ATTN_ROWS)
        q = q_ref[rows, :]
        bias = bias_ref[rows, :]
        v = v_ref[...]
        for m in range(2):
            s = lax.dot_general(q[:, m * dh:(m + 1) * dh], k_ref[:, m * dh:(m + 1) * dh],
                                (((1,), (1,)), ((), ())), preferred_element_type=F32) + bias
            m_prev = m_ref[m, rows, :]
            m_new = jnp.maximum(m_prev, jnp.max(s, axis=-1, keepdims=True))
            alpha = jnp.exp2(m_prev - m_new)
            p = jnp.exp2(s - _lane_tile(m_new, tk // LANES))
            l_ref[m, rows, :] = alpha * l_ref[m, rows, :] + jnp.sum(p, axis=-1, keepdims=True)
            acc_ref[m, rows, :] = (_lane_tile(alpha, rep) * acc_ref[m, rows, :]
                                   + jnp.dot(p.astype(v.dtype), v, preferred_element_type=F32))
            m_ref[m, rows, :] = m_new
        return carry

    lax.fori_loop(0, tq // ATTN_ROWS, rows_block, 0, unroll=ATTN_UNROLL)

    @pl.when(j == nkv - 1)
    def _():
        lam_p = lam_ref[...]
        lam = (jnp.exp(jnp.sum(lam_p[0:1] * lam_p[1:2], axis=-1, keepdims=True))
               - jnp.exp(jnp.sum(lam_p[2:3] * lam_p[3:4], axis=-1, keepdims=True)) + lambda_init)
        o = (acc_ref[0] / _lane_tile(l_ref[0], rep) - lam * (acc_ref[1] / _lane_tile(l_ref[1], rep)))
        o = o * lax.rsqrt(jnp.mean(o * o, axis=-1, keepdims=True) + EPS) * g_ref[...]
        o_ref[...] = (o * (1.0 - lambda_init)).astype(o_ref.dtype)


def diff_attention_core(qkv, bias, lam_params, subln_g, lambda_init, t):
    bsz, L, D3 = qkv.shape
    D = D3 // 3
    hw = 2 * DA_HEAD_DIM
    H = D // hw
    nq = L // t
    return pl.pallas_call(
        functools.partial(_attn_kernel, nkv=nq, lambda_init=lambda_init),
        grid=(bsz, H, nq, nq),
        in_specs=[pl.BlockSpec((None, t, hw), lambda b, h, i, j: (b, i, h)),
                  pl.BlockSpec((None, t, hw), lambda b, h, i, j: (b, j, H + h)),
                  pl.BlockSpec((None, t, hw), lambda b, h, i, j: (b, j, 2 * H + h)),
                  pl.BlockSpec((None, None, t, t),
                               lambda b, h, i, j: (h, jnp.clip(j - i, -2, 2) + 2, 0, 0)),
                  pl.BlockSpec((4, DA_HEAD_DIM), lambda b, h, i, j: (0, 0)),
                  pl.BlockSpec((1, hw), lambda b, h, i, j: (0, 0))],
        out_specs=pl.BlockSpec((None, t, hw), lambda b, h, i, j: (b, i, h)),
        out_shape=jax.ShapeDtypeStruct((bsz, L, D), BF16),
        scratch_shapes=[pltpu.VMEM((2, t, LANES), F32), pltpu.VMEM((2, t, LANES), F32),
                        pltpu.VMEM((2, t, hw), F32)],
        compiler_params=_params(("parallel", "parallel", "parallel", "arbitrary")),
        name="diff_attention",
    )(qkv, qkv, qkv, bias, lam_params, subln_g.reshape(1, hw).astype(F32))


def _hgrn_kernel(q_ref, f_ref, i_ref, gate_ref, lb_ref, ng_ref, tri_ref, o_ref, ofw_ref, s_ref, g_scr, kk_scr, *,
                 tl, n_tiles, hb):
    dr = pl.program_id(2)
    t = pl.program_id(3)
    C = HG_CHUNK
    nch = tl // C
    W = hb * HG_DIM
    fwd = dr == 0

    @pl.when(t == 0)
    def _():
        s_ref[...] = jnp.zeros(s_ref.shape, F32)

    lb = lb_ref[...]
    sig = jax.nn.sigmoid(f_ref[...])
    logf = jnp.log(lb + (1.0 - lb) * sig)
    kk = (1.0 - lb) * (1.0 - sig)
    hi = logf.astype(BF16)
    lo = (logf - hi.astype(F32)).astype(BF16)
    tri = tri_ref[...]
    g = (jnp.dot(tri, hi, preferred_element_type=F32) + jnp.dot(tri, lo, preferred_element_type=F32))
    g3 = g.reshape(nch, C, W)
    g_tot = jnp.sum(logf.reshape(nch, C, W), axis=1, keepdims=True)
    qt = (q_ref[...] * jnp.exp(g)).astype(BF16)
    kt = (kk * jnp.exp(-g)).astype(BF16)
    kd = (kk.reshape(nch, C, W) * jnp.exp(g_tot - g3)).reshape(tl, W).astype(BF16)
    dec = jnp.exp(g_tot).reshape(nch, W)
    inp = i_ref[...].astype(BF16)
    row = lax.broadcasted_iota(jnp.int32, (C, C), 0)
    col = lax.broadcasted_iota(jnp.int32, (C, C), 1)
    causal = jnp.where(fwd, row - col, col - row) >= 0
    tile_idx = jnp.where(fwd, t, n_tiles - 1 - t)
    base = pl.multiple_of(tile_idx * tl, C)
    nt_dims = (((1,), (1,)), ((), ()))
    tn_dims = (((0,), (0,)), ((), ()))

    def att_robust(c, cs):
        r = slice(c * C, (c + 1) * C)
        qf, gc = q_ref[r, cs], g[r, cs]

        def body(blk, att):
            r0 = pl.multiple_of(c * C + blk * 8, 8)
            g8, k8 = g_scr[pl.ds(r0, 8), cs], kk_scr[pl.ds(r0, 8), cs]
            for u in range(8):
                e = jnp.exp(jnp.minimum(gc - g8[u:u + 1], 0.0))
                att = jnp.where(col == blk * 8 + u,
                                jnp.sum(qf * e * k8[u:u + 1], axis=-1, keepdims=True), att)
            return att

        return lax.fori_loop(0, C // 8, body, jnp.zeros((C, C), F32))

    def run(order, is_fwd, robust):
        for c in order:
            r = slice(c * C, (c + 1) * C)
            for hh in range(hb):
                cs = slice(hh * HG_DIM, (hh + 1) * HG_DIM)
                qc, kc, ic, kdc = qt[r, cs], kt[r, cs], inp[r, cs], kd[r, cs]
                if robust:
                    att = att_robust(c, cs)
                else:
                    att = lax.dot_general(qc, kc, nt_dims, preferred_element_type=F32)
                att = jnp.where(causal, att, 0.0).astype(BF16)
                st = s_ref[hh]
                o = (jnp.dot(att, ic, preferred_element_type=F32)
                     + lax.dot_general(qc, st.astype(BF16), nt_dims, preferred_element_type=F32))
                s_ref[hh] = (dec[c:c + 1, cs] * st
                             + lax.dot_general(ic, kdc, tn_dims, preferred_element_type=F32))
                rows = pl.ds(pl.multiple_of(base + c * C, C), C)
                if is_fwd:
                    ofw_ref[rows, cs] = o
                else:
                    tot = ofw_ref[rows, cs] + o
                    y = tot * lax.rsqrt(jnp.mean(tot * tot, axis=-1, keepdims=True) + EPS) * ng_ref[...]
                    gt = gate_ref[r, cs]
                    o_ref[r, cs] = (y * (gt * jax.nn.sigmoid(gt))).astype(o_ref.dtype)

    risky = jnp.min(g_tot) < HG_SAFE_LOG_DECAY
    safe = jnp.logical_not(risky)

    @pl.when(risky)
    def _():
        g_scr[...] = g
        kk_scr[...] = kk

    for is_fwd, order in ((True, range(nch)), (False, range(nch - 1, -1, -1))):
        direction = fwd if is_fwd else jnp.logical_not(fwd)
        pl.when(direction & safe)(functools.partial(run, order, is_fwd, False))
        pl.when(direction & risky)(functools.partial(run, order, is_fwd, True))


def hgrn2_core(proj, lb, norm_g, *, tl=512, hb=HG_HEADS_PER_STEP):
    bsz, L, D5 = proj.shape
    D = D5 // 5
    H = D // HG_DIM
    hb = min(hb, H)
    HB = H // hb
    W = hb * HG_DIM
    tl = _tile(L, tl)
    nt = L // tl
    C = HG_CHUNK
    idx = jnp.arange(tl, dtype=jnp.int32)
    same = (idx[:, None] // C) == (idx[None, :] // C)
    tri = jnp.stack([same & (idx[None, :] <= idx[:, None]),
                     same & (idx[None, :] >= idx[:, None])]).astype(BF16)

    def tile_of(dr, t):
        return jnp.where(dr == 0, t, nt - 1 - t)

    def in_spec(section):
        return pl.BlockSpec((None, tl, W), lambda b, h, dr, t: (b, tile_of(dr, t), section * HB + h))

    f_spec = pl.BlockSpec((None, tl, W), lambda b, h, dr, t: (b, tile_of(dr, t), (1 + dr) * HB + h))
    out_spec = pl.BlockSpec((None, tl, W),
                            lambda b, h, dr, t: (b, jnp.where(dr == 0, nt - 1, nt - 1 - t), h))
    return pl.pallas_call(
        functools.partial(_hgrn_kernel, tl=tl, n_tiles=nt, hb=hb),
        grid=(bsz, HB, 2, nt),
        in_specs=[in_spec(0), f_spec, in_spec(3), in_spec(4),
                  pl.BlockSpec((1, W), lambda b, h, dr, t: (0, h)),
                  pl.BlockSpec((1, HG_DIM), lambda b, h, dr, t: (0, 0)),
                  pl.BlockSpec((None, tl, tl), lambda b, h, dr, t: (dr, 0, 0))],
        out_specs=out_spec,
        out_shape=jax.ShapeDtypeStruct((bsz, L, D), BF16),
        scratch_shapes=[pltpu.VMEM((L, W), F32), pltpu.VMEM((hb, HG_DIM, HG_DIM), F32),
                        pltpu.VMEM((tl, W), F32), pltpu.VMEM((tl, W), F32)],
        compiler_params=_params(("parallel", "parallel", "arbitrary", "arbitrary")),
        name="hgrn2",
    )(proj, proj, proj, proj, lb.reshape(1, D).astype(F32), norm_g.reshape(1, HG_DIM).astype(F32), tri)


def _pool_kernel(h_ref, o_ref, pad_ref, *, L):
    grp = pl.program_id(1)
    x = h_ref[...].astype(F32)
    zeros = jnp.zeros((POOL_HALO, x.shape[1]), F32)
    pad_ref[0:POOL_HALO, :] = zeros
    pad_ref[POOL_HALO:POOL_HALO + L, :] = x
    pad_ref[POOL_HALO + L:POOL_HALO + L + POOL_HALO, :] = zeros
    pos = lax.broadcasted_iota(jnp.int32, (L, 1), 0)
    for gi, w in enumerate(POOL_WINDOWS):
        @pl.when(grp == gi)
        def _(w=w):
            acc = jnp.zeros(x.shape, F32)
            for d in range(-(w // 2), w - w // 2):
                acc = acc + pad_ref[POOL_HALO + d:POOL_HALO + d + L, :]
            lo = jnp.clip(pos - w // 2, 0, L)
            hi = jnp.clip(pos + w - w // 2, 0, L)
            cnt = (hi - lo).astype(F32)
            o_ref[...] = (acc / cnt - x).astype(o_ref.dtype)


def pool_core(h, *, width=128):
    bsz, L, D = h.shape
    ng = len(POOL_WINDOWS)
    dg = D // ng
    width = _tile(dg, width)
    per_g = dg // width
    spec = pl.BlockSpec((None, L, width), lambda b, g, s: (b, 0, g * per_g + s))
    return pl.pallas_call(
        functools.partial(_pool_kernel, L=L),
        grid=(bsz, ng, per_g),
        in_specs=[spec], out_specs=spec,
        out_shape=jax.ShapeDtypeStruct((bsz, L, D), BF16),
        scratch_shapes=[pltpu.VMEM((L + 2 * POOL_HALO, width), F32)],
        compiler_params=_params(("parallel", "parallel", "parallel")),
        name="pool",
    )(h)


def _trunk(x, c, p, shared):
    bsz, L, D = x.shape
    T = bsz * L
    depth = p["norm1_g"].shape[0]
    c_low = small_matmul(c, p["w_ada_down"], silu_in=True)
    for l in range(depth):
        mod = small_matmul(c_low, p["w_ada"][l], p["b_ada"][l]).reshape(bsz, 6, 1, D)
        h = norm_mod(x, p["norm1_g"][l], mod, 1, 0)
        kind, j = l % 4, l // 4
        x2 = x.reshape(T, D)
        if kind == 0:
            z = s5_mix(h, shared["s5_tables"][j], shared["s5_scan"][(j, L)], p["s5_d"][j])
            x = glu_linear(z.reshape(T, D), shared["s5_w_glu"][j], "sigglu", out_dtype=F32,
                           res=x2, gate=mod, gate_idx=2, rows_per_batch=L, name="s5_glu")
        elif kind == 1:
            lambda_init = 0.8 - 0.6 * math.exp(-0.3 * l)
            qkv = linear(h.reshape(T, D), shared["da_w_qkv"][j], name="da_qkv")
            lam_params = jnp.stack([p["da_lam_q1"][j], p["da_lam_k1"][j],
                                    p["da_lam_q2"][j], p["da_lam_k2"][j]]).astype(F32)
            o = diff_attention_core(qkv.reshape(bsz, L, 3 * D), shared["bias_tiles"], lam_params,
                                    p["da_subln_g"][j], lambda_init, shared["attn_tile"])
            x = linear(o.reshape(T, D), shared["da_w_o"][j], out_dtype=F32, res=x2, gate=mod,
                       gate_idx=2, rows_per_batch=L, name="da_out")
        elif kind == 2:
            proj = linear(h.reshape(T, D), shared["hg_w_in"][j], out_dtype=F32, name="hg_in")
            o = hgrn2_core(proj.reshape(bsz, L, 5 * D), shared["hg_lb"][l], p["hg_norm_g"][j])
            x = linear(o.reshape(T, D), shared["hg_w_o"][j], out_dtype=F32, res=x2, gate=mod,
                       gate_idx=2, rows_per_batch=L, name="hg_out")
        else:
            pooled = pool_core(h)
            x = pool_linear(pooled.reshape(T, D), shared["pool_w"][j], p["pool_scale"][j].astype(F32),
                            x2, mod, 2, L)
        x = x.reshape(bsz, L, D)
        x2 = x.reshape(T, D)
        if l % 2 == 0:
            h = norm_mod(x, p["norm2_g"][l], mod, 4, 3)
            hid = glu_linear(h.reshape(T, D), shared["ff_w_in"][l // 2], "swiglu", name="ff_in")
            x = linear(hid, shared["ff_w_out"][l // 2], tk=_ff_tk(hid.shape[1]), out_dtype=F32,
                       res=x2, gate=mod, gate_idx=5, rows_per_batch=L, name="ff_out")
        else:
            h, route, counts = norm_mod(x, p["norm2_g"][l], mod, 4, 3, w_router=p["moe_router"][l // 2])
            x = moe_ffn(h.reshape(T, D), route.reshape(T, LANES), counts, shared["moe_w_in"][l // 2],
                        shared["moe_w_out"][l // 2], x2, mod, 5, L)
        x = x.reshape(bsz, L, D)
    return final_norm(x, p["final_g"])


def _ff_tk(k):
    return k // 2 if (k // 2) % LANES == 0 else k


def kernel(x_prompt, x_sample, c_prompt, c_sample, norm1_g, norm2_g, final_g, w_ada_down, w_ada, b_ada,
           s5_a_re, s5_a_im, s5_log_dt, s5_b_re, s5_b_im, s5_c_re, s5_c_im, s5_d, s5_w_glu,
           da_w_qkv, da_w_o, da_lam_q1, da_lam_k1, da_lam_q2, da_lam_k2, da_subln_g, rel_bias,
           hg_w_in, hg_w_o, hg_norm_g, hg_lb_logits,
           pool_w, pool_scale,
           ff_w_in, ff_w_out, moe_router, moe_w_in, moe_w_out):
    p = dict(norm1_g=norm1_g, norm2_g=norm2_g, final_g=final_g, w_ada_down=w_ada_down, w_ada=w_ada,
             b_ada=b_ada, s5_d=s5_d, da_lam_q1=da_lam_q1, da_lam_k1=da_lam_k1, da_lam_q2=da_lam_q2,
             da_lam_k2=da_lam_k2, da_subln_g=da_subln_g, hg_norm_g=hg_norm_g, pool_scale=pool_scale,
             moe_router=moe_router)
    seq_lens = sorted({x_prompt.shape[1], x_sample.shape[1]})
    attn_tile = _tile(min(seq_lens), 512)
    lb_cum = jnp.cumsum(jax.nn.softmax(hg_lb_logits.astype(F32), axis=0), axis=0)
    hg_lb = jnp.concatenate([jnp.zeros_like(lb_cum[:1]), lb_cum[:-1]], axis=0)
    s5_scan = {}
    for j in range(s5_a_re.shape[0]):
        for L in seq_lens:
            nlev = max(1, math.ceil(math.log2(L // S5_CHUNK)))
            s5_scan[(j, L)] = _s5_scan_tables(s5_a_re[j], s5_a_im[j], s5_log_dt[j], nlev)
    d_model = x_prompt.shape[-1]
    qkv_scale = jnp.concatenate([jnp.full((d_model,), DA_HEAD_DIM ** -0.5 * LOG2E, F32),
                                 jnp.ones((2 * d_model,), F32)])
    shared = dict(
        s5_tables=[_s5_tables(s5_a_re[j], s5_a_im[j], s5_log_dt[j], s5_b_re[j], s5_b_im[j],
                              s5_c_re[j], s5_c_im[j]) for j in range(s5_a_re.shape[0])],
        s5_scan=s5_scan,
        s5_w_glu=s5_w_glu.astype(BF16), da_w_qkv=(da_w_qkv * qkv_scale).astype(BF16),
        da_w_o=da_w_o.astype(BF16),
        hg_w_in=hg_w_in.astype(BF16), hg_w_o=hg_w_o.astype(BF16), pool_w=pool_w.astype(BF16),
        ff_w_in=ff_w_in.astype(BF16), ff_w_out=ff_w_out.astype(BF16),
        moe_w_in=moe_w_in.astype(BF16), moe_w_out=moe_w_out.astype(BF16),
        hg_lb=hg_lb, attn_tile=attn_tile, bias_tiles=bias_tiles(rel_bias, attn_tile),
    )
    y_prompt = _trunk(x_prompt, c_prompt, p, shared)
    y_sample = _trunk(x_sample, c_sample, p, shared)
    return (y_prompt, y_sample)
```
